```python
import math
import jax
import jax.numpy as jnp
from jax import lax
import numpy as np


D_MODEL = 2048
BATCH = 4
SEQ = 2048
DEPTH = 2

N_BRANCH = 4
BR_W = D_MODEL // 4
RMS_EPS = 1e-6

HY_W = BR_W
HY_SHORT = 3
HY_EMB = 33
HY_BANDS = (HY_EMB - 1) // 2
HY_ORDER = 64
HY_TARGET = 1e-2
HY_FAST = 0.3
HY_SLOW = 1.5

LRU_W = BR_W
LRU_HEADS = 8
LRU_HD = LRU_W // LRU_HEADS
LRU_CONV = 4
LRU_C = 8.0

DA_HEADS = 4
DA_HD = BR_W // (2 * DA_HEADS)
DA_QBLOCK = 128
REL_BUCKETS = 32
REL_MAX_DIST = 128

SSD_W = BR_W
SSD_HD = 64
SSD_HEADS = SSD_W // SSD_HD
SSD_GROUPS = 2
SSD_STATE = 128
SSD_CONV = 4
SSD_CHUNK = 128

N_EXPERTS = 16
EC_CAPACITY = 2
D_EXPERT = D_MODEL

HY_COLS = 3 * HY_W
LRU_COLS = 2 * LRU_W
DA_COLS = 3 * DA_HEADS * 2 * DA_HD
SSD_XBC = SSD_W + 2 * SSD_GROUPS * SSD_STATE
SSD_COLS = SSD_W + SSD_XBC + 2 * SSD_HEADS
GATE_COLS = N_BRANCH * D_MODEL
IN_COLS = HY_COLS + LRU_COLS + DA_COLS + SSD_COLS + GATE_COLS

kernel_name = 'hybrid_gated_hyena_rglru_diffattn_ssd_ecmoe'


def _rmsnorm(x, g, eps=RMS_EPS):
    xf = x.astype(jnp.float32)
    y = xf * lax.rsqrt(jnp.mean(xf * xf, axis=-1, keepdims=True) + eps)
    return (y * g.astype(jnp.float32)).astype(x.dtype)


def _dwconv_centred(x, w, b):
    k = w.shape[0]
    left = k // 2
    right = k - 1 - left
    y = lax.conv_general_dilated(x, w[:, None, :].astype(x.dtype), window_strides=(1,),
                                 padding=[(left, right)], dimension_numbers=('NWC', 'WIO', 'NWC'),
                                 feature_group_count=x.shape[-1])
    return y + b.astype(x.dtype)


def _hyena_filter(seq_len, fw1, fb1, fw2, fb2, fw3, fb3, fw4, freq):
    f32 = jnp.float32
    t = jnp.linspace(0.0, 1.0, seq_len, dtype=f32)[:, None]
    w = (2.0 * math.pi / seq_len) * jnp.arange(seq_len, dtype=f32)[:, None]
    bands = jnp.linspace(1e-4, HY_BANDS - 1, HY_BANDS, dtype=f32)[None, :]
    z = jnp.concatenate([t, jnp.cos(bands * w), -jnp.sin(bands * w)], axis=-1).astype(fw1.dtype)
    hid = jnp.sin(freq * (z @ fw1 + fb1))
    hid = jnp.sin(freq * (hid @ fw2 + fb2))
    hid = jnp.sin(freq * (hid @ fw3 + fb3))
    filt = (hid @ fw4).astype(f32)
    deltas = jnp.abs(jnp.linspace(math.log(HY_TARGET) / HY_SLOW, math.log(HY_TARGET) / HY_FAST,
                                  HY_W, dtype=f32))
    filt = filt * jnp.exp(-2.0 * jnp.abs(t - 0.5) * deltas[None, :])
    return filt / jnp.sum(jnp.abs(filt), axis=0, keepdims=True)


def _hyena_mixer(u, conv_w, conv_b, fw1, fb1, fw2, fb2, fw3, fb3, fw4, freq, fft_bias):
    _, seq_len, _ = u.shape
    x0, x1, v = jnp.split(_dwconv_centred(u, conv_w, conv_b), 3, axis=-1)
    v = (v * x1).astype(jnp.float32)
    filt = _hyena_filter(seq_len, fw1, fb1, fw2, fb2, fw3, fb3, fw4, freq)
    n_fft = 2 * seq_len
    y = jnp.fft.irfft(jnp.fft.rfft(v, n=n_fft, axis=1) * jnp.fft.rfft(filt, n=n_fft, axis=0)[None],
                      n=n_fft, axis=1)
    start = seq_len // 2
    y = y[:, start:start + seq_len] + v * fft_bias.astype(jnp.float32)
    return y.astype(u.dtype) * x0


def _linear_combine(e1, e2):
    a1, b1 = e1
    a2, b2 = e2
    return a1 * a2, a2 * b1 + b2


def _rglru_direction(xc, w_a, b_a, w_x, b_x, lam, reverse):
    bsz, seq_len, width = xc.shape
    xh = xc.reshape(bsz, seq_len, LRU_HEADS, LRU_HD)
    gate_a = jnp.einsum('blhi,hij->blhj', xh, w_a).reshape(bsz, seq_len, width) + b_a
    gate_x = jnp.einsum('blhi,hij->blhj', xh, w_x).reshape(bsz, seq_len, width) + b_x
    r = jax.nn.sigmoid(gate_a.astype(jnp.float32))
    i = jax.nn.sigmoid(gate_x.astype(jnp.float32))
    log_a = -LRU_C * r * jax.nn.softplus(-lam.astype(jnp.float32))
    a = jnp.exp(log_a)
    mult = jnp.sqrt(-jnp.expm1(2.0 * log_a))
    first = seq_len - 1 if reverse else 0
    mult = jnp.where((jnp.arange(seq_len) == first)[None, :, None], 1.0, mult)
    b = mult * i * xc.astype(jnp.float32)
    _, h = lax.associative_scan(_linear_combine, (a, b), reverse=reverse, axis=1)
    return h


def _rglru_mixer(u_x, u_gate, conv_w, conv_b, w_a, b_a, w_x, b_x, lam):
    xc = _dwconv_centred(u_x, conv_w, conv_b)
    h_f = _rglru_direction(xc, w_a[0], b_a[0], w_x[0], b_x[0], lam[0], False)
    h_b = _rglru_direction(xc, w_a[1], b_a[1], w_x[1], b_x[1], lam[1], True)
    return (h_f + h_b).astype(u_x.dtype) * jax.nn.gelu(u_gate)


def _t5_bucket(rel):
    nb = REL_BUCKETS // 2
    ret = (rel > 0).astype(jnp.int32) * nb
    n = jnp.abs(rel)
    max_exact = nb // 2
    large = max_exact + (jnp.log(jnp.maximum(n, 1).astype(jnp.float32) / max_exact)
                         / math.log(REL_MAX_DIST / max_exact) * (nb - max_exact)).astype(jnp.int32)
    large = jnp.minimum(large, nb - 1)
    return ret + jnp.where(n < max_exact, n, large)


def _diff_attention(q, k, v, lam_qk, subln_g, rel_bias, lam_init):
    bsz, seq_len, _ = q.shape
    q = q.reshape(bsz, seq_len, DA_HEADS, 2, DA_HD).transpose(0, 2, 3, 1, 4)
    k = k.reshape(bsz, seq_len, DA_HEADS, 2, DA_HD).transpose(0, 2, 3, 1, 4)
    v = v.reshape(bsz, seq_len, DA_HEADS, 2 * DA_HD).transpose(0, 2, 1, 3)
    lq = lam_qk.astype(jnp.float32)
    lam = jnp.exp(jnp.sum(lq[0] * lq[1])) - jnp.exp(jnp.sum(lq[2] * lq[3])) + lam_init
    n_blk = seq_len // DA_QBLOCK
    q_blocks = jnp.moveaxis(q.reshape(bsz, DA_HEADS, 2, n_blk, DA_QBLOCK, DA_HD), 3, 0)
    kpos = jnp.arange(seq_len)
    scale = DA_HD ** -0.5

    def block(args):
        q_blk, start = args
        qpos = start + jnp.arange(DA_QBLOCK)
        bias = rel_bias[_t5_bucket(kpos[None, :] - qpos[:, None])]
        bias = jnp.transpose(bias, (2, 0, 1)).astype(jnp.float32)
        s = jnp.einsum('bhiqd,bhikd->bhiqk', q_blk, k).astype(jnp.float32) * scale + bias[None, :, None]
        p = jax.nn.softmax(s, axis=-1)
        a = p[:, :, 0] - lam * p[:, :, 1]
        return jnp.einsum('bhqk,bhkd->bhqd', a.astype(v.dtype), v)

    o = lax.map(block, (q_blocks, jnp.arange(n_blk) * DA_QBLOCK))
    o = jnp.moveaxis(o, 0, 2).reshape(bsz, DA_HEADS, seq_len, 2 * DA_HD)
    o = _rmsnorm(o, subln_g, 1e-5) * (1.0 - lam_init)
    return o.transpose(0, 2, 1, 3).reshape(bsz, seq_len, DA_HEADS * 2 * DA_HD)


def _ssd_chunked(x, dt, a, bm, cm):
    bsz, seq_len, n_heads, hd = x.shape
    g = bm.shape[2]
    j = n_heads // g
    n = bm.shape[-1]
    q = SSD_CHUNK
    c = seq_len // q
    xd = (x.astype(jnp.float32) * dt[..., None]).reshape(bsz, c, q, g, j, hd)
    a_cum = jnp.cumsum((dt * a).reshape(bsz, c, q, g, j), axis=2)
    bc = bm.astype(jnp.float32).reshape(bsz, c, q, g, n)
    cc = cm.astype(jnp.float32).reshape(bsz, c, q, g, n)
    seg = a_cum[:, :, :, None] - a_cum[:, :, None, :]
    lower = jnp.tril(jnp.ones((q, q), dtype=bool))[:, :, None, None]
    l_mat = jnp.exp(jnp.where(lower, seg, -jnp.inf))
    cb = jnp.einsum('bclgn,bcsgn->bclsg', cc, bc)
    y_diag = jnp.einsum('bclsgj,bcsgjp->bclgjp', cb[..., None] * l_mat, xd)
    decay_s = jnp.exp(a_cum[:, :, -1:] - a_cum)
    states = jnp.einsum('bclgn,bclgjp->bcgjpn', bc, xd * decay_s[..., None])
    chunk_decay = jnp.exp(a_cum[:, :, -1])

    def step(s, inp):
        dec, st = inp
        return dec[..., None, None] * s + st, s

    s0 = jnp.zeros((bsz, g, j, hd, n), jnp.float32)
    _, prev = lax.scan(step, s0, (jnp.moveaxis(chunk_decay, 1, 0), jnp.moveaxis(states, 1, 0)))
    prev = jnp.moveaxis(prev, 0, 1)
    y_off = jnp.einsum('bclgn,bcgjpn->bclgjp', cc, prev) * jnp.exp(a_cum)[..., None]
    return (y_diag + y_off).reshape(bsz, seq_len, n_heads, hd)


def _ssd_mixer(z, xbc, dt_raw, conv_w, conv_b, dt_bias, a_log, d_skip, norm_g):
    bsz, seq_len, _ = z.shape
    xbc = jax.nn.silu(_dwconv_centred(xbc, conv_w, conv_b))
    xs, bm, cm = jnp.split(xbc, [SSD_W, SSD_W + SSD_GROUPS * SSD_STATE], axis=-1)
    xh = xs.reshape(bsz, seq_len, SSD_HEADS, SSD_HD)
    bm = bm.reshape(bsz, seq_len, SSD_GROUPS, SSD_STATE)
    cm = cm.reshape(bsz, seq_len, SSD_GROUPS, SSD_STATE)
    dts = jax.nn.softplus(dt_raw.astype(jnp.float32).reshape(bsz, seq_len, 2, SSD_HEADS)
                          + dt_bias.astype(jnp.float32))
    a = -jnp.exp(a_log.astype(jnp.float32))
    y_f = _ssd_chunked(xh, dts[:, :, 0], a[0], bm, cm)
    fl = lambda t: jnp.flip(t, axis=1)
    y_b = fl(_ssd_chunked(fl(xh), fl(dts[:, :, 1]), a[1], fl(bm), fl(cm)))
    y = y_f + y_b + d_skip.astype(jnp.float32)[:, None] * xh.astype(jnp.float32)
    y = y.reshape(bsz, seq_len, SSD_W) * jax.nn.silu(z.astype(jnp.float32))
    yg = y.reshape(bsz, seq_len, SSD_GROUPS, SSD_W // SSD_GROUPS)
    yg = yg * lax.rsqrt(jnp.mean(yg * yg, axis=-1, keepdims=True) + RMS_EPS)
    return (yg.reshape(bsz, seq_len, SSD_W) * norm_g.astype(jnp.float32)).astype(z.dtype)


def _expert_choice_ffn(h, w_router, w1, w3, w2):
    bsz, n_tok, _ = h.shape
    cap = EC_CAPACITY * n_tok // N_EXPERTS
    aff = jax.nn.softmax((h @ w_router).astype(jnp.float32), axis=-1)
    gate, idx = lax.top_k(jnp.swapaxes(aff, 1, 2), cap)
    bidx = jnp.arange(bsz)[:, None, None]
    xg = h[bidx, idx]
    hid = jax.nn.silu(jnp.einsum('becd,edf->becf', xg, w1)) * jnp.einsum('becd,edf->becf', xg, w3)
    y = jnp.einsum('becf,efd->becd', hid, w2) * gate[..., None].astype(h.dtype)
    return jnp.zeros_like(h).at[bidx, idx].add(y)


def _normal(k, shape, scale):
    return scale * jax.random.normal(k, shape, jnp.float32)


def setup_inputs(seed: int = 0) -> dict:
    key = jax.random.key(seed)
    ks = iter(jax.random.split(key, 48))
    L = DEPTH
    dt0 = jnp.exp(jax.random.uniform(next(ks), (L, 2, SSD_HEADS), jnp.float32,
                                     math.log(1e-3), math.log(1e-1)))
    u_lam = jax.random.uniform(next(ks), (L, 2, LRU_W), jnp.float32, 0.9, 0.999)
    s_lam = u_lam ** (1.0 / LRU_C)
    return {
        'x': _normal(next(ks), (BATCH, SEQ, D_MODEL), 1.0),
        'norm1_g': 1.0 + _normal(next(ks), (L, D_MODEL), 0.05),
        'w_in': _normal(next(ks), (L, D_MODEL, IN_COLS), D_MODEL ** -0.5),
        'hy_conv_w': _normal(next(ks), (L, HY_SHORT, HY_COLS), HY_SHORT ** -0.5),
        'hy_conv_b': _normal(next(ks), (L, HY_COLS), 0.02),
        'hy_fw1': _normal(next(ks), (L, HY_EMB, HY_ORDER), HY_EMB ** -0.5),
        'hy_fb1': _normal(next(ks), (L, HY_ORDER), 0.1),
        'hy_fw2': _normal(next(ks), (L, HY_ORDER, HY_ORDER), HY_ORDER ** -0.5),
        'hy_fb2': _normal(next(ks), (L, HY_ORDER), 0.1),
        'hy_fw3': _normal(next(ks), (L, HY_ORDER, HY_ORDER), HY_ORDER ** -0.5),
        'hy_fb3': _normal(next(ks), (L, HY_ORDER), 0.1),
        'hy_fw4': _normal(next(ks), (L, HY_ORDER, HY_W), HY_ORDER ** -0.5),
        'hy_freq': 1.0 + _normal(next(ks), (L, HY_ORDER), 0.1),
        'hy_bias': _normal(next(ks), (L, HY_W), 1.0),
        'lru_conv_w': _normal(next(ks), (L, LRU_CONV, LRU_W), LRU_CONV ** -0.5),
        'lru_conv_b': _normal(next(ks), (L, LRU_W), 0.02),
        'lru_wa': _normal(next(ks), (L, 2, LRU_HEADS, LRU_HD, LRU_HD), LRU_HD ** -0.5),
        'lru_ba': _normal(next(ks), (L, 2, LRU_W), 0.02),
        'lru_wx': _normal(next(ks), (L, 2, LRU_HEADS, LRU_HD, LRU_HD), LRU_HD ** -0.5),
        'lru_bx': _normal(next(ks), (L, 2, LRU_W), 0.02),
        'lru_lam': jnp.log(s_lam) - jnp.log1p(-s_lam),
        'da_lam': _normal(next(ks), (L, 4, DA_HD), 0.1),
        'da_subln_g': 1.0 + _normal(next(ks), (L, 2 * DA_HD), 0.05),
        'ssd_conv_w': _normal(next(ks), (L, SSD_CONV, SSD_XBC), SSD_CONV ** -0.5),
        'ssd_conv_b': _normal(next(ks), (L, SSD_XBC), 0.02),
        'ssd_dt_bias': dt0 + jnp.log(-jnp.expm1(-dt0)),
        'ssd_a_log': jnp.log(jax.random.uniform(next(ks), (L, 2, SSD_HEADS), jnp.float32, 1.0, 16.0)),
        'ssd_d': 1.0 + _normal(next(ks), (L, SSD_HEADS), 0.1),
        'ssd_norm_g': 1.0 + _normal(next(ks), (L, SSD_W), 0.05),
        'w_branch': _normal(next(ks), (L, N_BRANCH, BR_W, D_MODEL), BR_W ** -0.5),
        'b_gate': _normal(next(ks), (L, N_BRANCH, D_MODEL), 0.02),
        'w_out': _normal(next(ks), (L, D_MODEL, D_MODEL), D_MODEL ** -0.5),
        'norm2_g': 1.0 + _normal(next(ks), (L, D_MODEL), 0.05),
        'w_router': _normal(next(ks), (L, D_MODEL, N_EXPERTS), D_MODEL ** -0.5),
        'moe_w1': _normal(next(ks), (L, N_EXPERTS, D_MODEL, D_EXPERT), D_MODEL ** -0.5),
        'moe_w3': _normal(next(ks), (L, N_EXPERTS, D_MODEL, D_EXPERT), D_MODEL ** -0.5),
        'moe_w2': _normal(next(ks), (L, N_EXPERTS, D_EXPERT, D_MODEL), D_EXPERT ** -0.5),
        'rel_bias': _normal(next(ks), (REL_BUCKETS, DA_HEADS), 0.5),
        'final_g': 1.0 + _normal(next(ks), (D_MODEL,), 0.05),
    }


def reference(x, norm1_g, w_in, hy_conv_w, hy_conv_b, hy_fw1, hy_fb1, hy_fw2, hy_fb2, hy_fw3, hy_fb3,
              hy_fw4, hy_freq, hy_bias, lru_conv_w, lru_conv_b, lru_wa, lru_ba, lru_wx, lru_bx, lru_lam,
              da_lam, da_subln_g, ssd_conv_w, ssd_conv_b, ssd_dt_bias, ssd_a_log, ssd_d, ssd_norm_g,
              w_branch, b_gate, w_out, norm2_g, w_router, moe_w1, moe_w3, moe_w2, rel_bias, final_g):
    bsz, seq_len, d_model = x.shape
    o1 = HY_COLS
    o2 = o1 + LRU_COLS
    o3 = o2 + DA_COLS
    o4 = o3 + SSD_COLS
    for l in range(DEPTH):
        h = _rmsnorm(x, norm1_g[l])
        proj = h @ w_in[l]
        p_hy, p_lru, p_da, p_ssd, p_gate = jnp.split(proj, [o1, o2, o3, o4], axis=-1)
        o_hy = _hyena_mixer(p_hy, hy_conv_w[l], hy_conv_b[l], hy_fw1[l], hy_fb1[l], hy_fw2[l], hy_fb2[l],
                            hy_fw3[l], hy_fb3[l], hy_fw4[l], hy_freq[l], hy_bias[l])
        lru_x, lru_gate = jnp.split(p_lru, 2, axis=-1)
        o_lru = _rglru_mixer(lru_x, lru_gate, lru_conv_w[l], lru_conv_b[l], lru_wa[l], lru_ba[l],
                             lru_wx[l], lru_bx[l], lru_lam[l])
        q, k, v = jnp.split(p_da, 3, axis=-1)
        lam_init = 0.8 - 0.6 * math.exp(-0.3 * l)
        o_da = _diff_attention(q, k, v, da_lam[l], da_subln_g[l], rel_bias, lam_init)
        z, xbc, dt_raw = jnp.split(p_ssd, [SSD_W, SSD_W + SSD_XBC], axis=-1)
        o_ssd = _ssd_mixer(z, xbc, dt_raw, ssd_conv_w[l], ssd_conv_b[l], ssd_dt_bias[l], ssd_a_log[l],
                           ssd_d[l], ssd_norm_g[l])
        branches = jnp.stack([o_hy, o_lru, o_da, o_ssd], axis=2)
        gates = jax.nn.sigmoid(p_gate.reshape(bsz, seq_len, N_BRANCH, d_model) + b_gate[l])
        merged = jnp.sum(gates * jnp.einsum('blkc,kcd->blkd', branches, w_branch[l]), axis=2)
        x = x + merged @ w_out[l]
        x = x + _expert_choice_ffn(_rmsnorm(x, norm2_g[l]), w_router[l], moe_w1[l], moe_w3[l], moe_w2[l])
    return _rmsnorm(x, final_g)
```

```python
import math
from functools import partial

import jax
import jax.numpy as jnp
from jax import lax
from jax.experimental import pallas as pl
from jax.experimental.pallas import tpu as pltpu

D_MODEL = 2048
BATCH = 4
SEQ = 2048
DEPTH = 2

N_BRANCH = 4
BR_W = D_MODEL // 4
RMS_EPS = 1e-6

HY_W = BR_W
HY_SHORT = 3
HY_EMB = 33
HY_BANDS = (HY_EMB - 1) // 2
HY_ORDER = 64
HY_TARGET = 1e-2
HY_FAST = 0.3
HY_SLOW = 1.5

LRU_W = BR_W
LRU_HEADS = 8
LRU_HD = LRU_W // LRU_HEADS
LRU_CONV = 4
LRU_C = 8.0

DA_HEADS = 4
DA_HD = BR_W // (2 * DA_HEADS)
DA_QBLOCK = 128
REL_BUCKETS = 32
REL_MAX_DIST = 128

SSD_W = BR_W
SSD_HD = 64
SSD_HEADS = SSD_W // SSD_HD
SSD_GROUPS = 2
SSD_STATE = 128
SSD_CONV = 4
SSD_CHUNK = 128

N_EXPERTS = 16
EC_CAPACITY = 2
D_EXPERT = D_MODEL

HY_COLS = 3 * HY_W
LRU_COLS = 2 * LRU_W
DA_COLS = 3 * DA_HEADS * 2 * DA_HD
SSD_XBC = SSD_W + 2 * SSD_GROUPS * SSD_STATE
SSD_COLS = SSD_W + SSD_XBC + 2 * SSD_HEADS
GATE_COLS = N_BRANCH * D_MODEL
IN_COLS = HY_COLS + LRU_COLS + DA_COLS + SSD_COLS + GATE_COLS


def _rmsnorm(x, g, eps=RMS_EPS):
    xf = x.astype(jnp.float32)
    y = xf * lax.rsqrt(jnp.mean(xf * xf, axis=-1, keepdims=True) + eps)
    return (y * g.astype(jnp.float32)).astype(x.dtype)


def _dwconv_centred(x, w, b):
    k = w.shape[0]
    left = k // 2
    right = k - 1 - left
    y = lax.conv_general_dilated(x, w[:, None, :].astype(x.dtype), window_strides=(1,),
                                 padding=[(left, right)], dimension_numbers=('NWC', 'WIO', 'NWC'),
                                 feature_group_count=x.shape[-1])
    return y + b.astype(x.dtype)


def _hyena_filter(seq_len, fw1, fb1, fw2, fb2, fw3, fb3, fw4, freq):
    f32 = jnp.float32
    t = jnp.linspace(0.0, 1.0, seq_len, dtype=f32)[:, None]
    w = (2.0 * math.pi / seq_len) * jnp.arange(seq_len, dtype=f32)[:, None]
    bands = jnp.linspace(1e-4, HY_BANDS - 1, HY_BANDS, dtype=f32)[None, :]
    z = jnp.concatenate([t, jnp.cos(bands * w), -jnp.sin(bands * w)], axis=-1).astype(fw1.dtype)
    hid = jnp.sin(freq * (z @ fw1 + fb1))
    hid = jnp.sin(freq * (hid @ fw2 + fb2))
    hid = jnp.sin(freq * (hid @ fw3 + fb3))
    filt = (hid @ fw4).astype(f32)
    deltas = jnp.abs(jnp.linspace(math.log(HY_TARGET) / HY_SLOW, math.log(HY_TARGET) / HY_FAST,
                                  HY_W, dtype=f32))
    filt = filt * jnp.exp(-2.0 * jnp.abs(t - 0.5) * deltas[None, :])
    return filt / jnp.sum(jnp.abs(filt), axis=0, keepdims=True)


def _hyena_mixer(u, conv_w, conv_b, fw1, fb1, fw2, fb2, fw3, fb3, fw4, freq, fft_bias):
    _, seq_len, _ = u.shape
    x0, x1, v = jnp.split(_dwconv_centred(u, conv_w, conv_b), 3, axis=-1)
    v = (v * x1).astype(jnp.float32)
    filt = _hyena_filter(seq_len, fw1, fb1, fw2, fb2, fw3, fb3, fw4, freq)
    n_fft = 2 * seq_len
    y = jnp.fft.irfft(jnp.fft.rfft(v, n=n_fft, axis=1) * jnp.fft.rfft(filt, n=n_fft, axis=0)[None],
                      n=n_fft, axis=1)
    start = seq_len // 2
    y = y[:, start:start + seq_len] + v * fft_bias.astype(jnp.float32)
    return y.astype(u.dtype) * x0


def _linear_combine(e1, e2):
    a1, b1 = e1
    a2, b2 = e2
    return a1 * a2, a2 * b1 + b2


def _rglru_direction(xc, w_a, b_a, w_x, b_x, lam, reverse):
    bsz, seq_len, width = xc.shape
    xh = xc.reshape(bsz, seq_len, LRU_HEADS, LRU_HD)
    gate_a = jnp.einsum('blhi,hij->blhj', xh, w_a).reshape(bsz, seq_len, width) + b_a
    gate_x = jnp.einsum('blhi,hij->blhj', xh, w_x).reshape(bsz, seq_len, width) + b_x
    r = jax.nn.sigmoid(gate_a.astype(jnp.float32))
    i = jax.nn.sigmoid(gate_x.astype(jnp.float32))
    log_a = -LRU_C * r * jax.nn.softplus(-lam.astype(jnp.float32))
    a = jnp.exp(log_a)
    mult = jnp.sqrt(-jnp.expm1(2.0 * log_a))
    first = seq_len - 1 if reverse else 0
    mult = jnp.where((jnp.arange(seq_len) == first)[None, :, None], 1.0, mult)
    b = mult * i * xc.astype(jnp.float32)
    _, h = lax.associative_scan(_linear_combine, (a, b), reverse=reverse, axis=1)
    return h


def _rglru_mixer(u_x, u_gate, conv_w, conv_b, w_a, b_a, w_x, b_x, lam):
    xc = _dwconv_centred(u_x, conv_w, conv_b)
    h_f = _rglru_direction(xc, w_a[0], b_a[0], w_x[0], b_x[0], lam[0], False)
    h_b = _rglru_direction(xc, w_a[1], b_a[1], w_x[1], b_x[1], lam[1], True)
    return (h_f + h_b).astype(u_x.dtype) * jax.nn.gelu(u_gate)


def _t5_bucket(rel):
    nb = REL_BUCKETS // 2
    ret = (rel > 0).astype(jnp.int32) * nb
    n = jnp.abs(rel)
    max_exact = nb // 2
    large = max_exact + (jnp.log(jnp.maximum(n, 1).astype(jnp.float32) / max_exact)
                         / math.log(REL_MAX_DIST / max_exact) * (nb - max_exact)).astype(jnp.int32)
    large = jnp.minimum(large, nb - 1)
    return ret + jnp.where(n < max_exact, n, large)


def _diff_attention(q, k, v, lam_qk, subln_g, rel_bias, lam_init):
    bsz, seq_len, _ = q.shape
    q = q.reshape(bsz, seq_len, DA_HEADS, 2, DA_HD).transpose(0, 2, 3, 1, 4)
    k = k.reshape(bsz, seq_len, DA_HEADS, 2, DA_HD).transpose(0, 2, 3, 1, 4)
    v = v.reshape(bsz, seq_len, DA_HEADS, 2 * DA_HD).transpose(0, 2, 1, 3)
    lq = lam_qk.astype(jnp.float32)
    lam = jnp.exp(jnp.sum(lq[0] * lq[1])) - jnp.exp(jnp.sum(lq[2] * lq[3])) + lam_init
    n_blk = seq_len // DA_QBLOCK
    q_blocks = jnp.moveaxis(q.reshape(bsz, DA_HEADS, 2, n_blk, DA_QBLOCK, DA_HD), 3, 0)
    kpos = jnp.arange(seq_len)
    scale = DA_HD ** -0.5

    def block(args):
        q_blk, start = args
        qpos = start + jnp.arange(DA_QBLOCK)
        bias = rel_bias[_t5_bucket(kpos[None, :] - qpos[:, None])]
        bias = jnp.transpose(bias, (2, 0, 1)).astype(jnp.float32)
        s = jnp.einsum('bhiqd,bhikd->bhiqk', q_blk, k).astype(jnp.float32) * scale + bias[None, :, None]
        p = jax.nn.softmax(s, axis=-1)
        a = p[:, :, 0] - lam * p[:, :, 1]
        return jnp.einsum('bhqk,bhkd->bhqd', a.astype(v.dtype), v)

    o = lax.map(block, (q_blocks, jnp.arange(n_blk) * DA_QBLOCK))
    o = jnp.moveaxis(o, 0, 2).reshape(bsz, DA_HEADS, seq_len, 2 * DA_HD)
    o = _rmsnorm(o, subln_g, 1e-5) * (1.0 - lam_init)
    return o.transpose(0, 2, 1, 3).reshape(bsz, seq_len, DA_HEADS * 2 * DA_HD)


def _ssd_chunked(x, dt, a, bm, cm):
    bsz, seq_len, n_heads, hd = x.shape
    g = bm.shape[2]
    j = n_heads // g
    n = bm.shape[-1]
    q = SSD_CHUNK
    c = seq_len // q
    xd = (x.astype(jnp.float32) * dt[..., None]).reshape(bsz, c, q, g, j, hd)
    a_cum = jnp.cumsum((dt * a).reshape(bsz, c, q, g, j), axis=2)
    bc = bm.astype(jnp.float32).reshape(bsz, c, q, g, n)
    cc = cm.astype(jnp.float32).reshape(bsz, c, q, g, n)
    seg = a_cum[:, :, :, None] - a_cum[:, :, None, :]
    lower = jnp.tril(jnp.ones((q, q), dtype=bool))[:, :, None, None]
    l_mat = jnp.exp(jnp.where(lower, seg, -jnp.inf))
    cb = jnp.einsum('bclgn,bcsgn->bclsg', cc, bc)
    y_diag = jnp.einsum('bclsgj,bcsgjp->bclgjp', cb[..., None] * l_mat, xd)
    decay_s = jnp.exp(a_cum[:, :, -1:] - a_cum)
    states = jnp.einsum('bclgn,bclgjp->bcgjpn', bc, xd * decay_s[..., None])
    chunk_decay = jnp.exp(a_cum[:, :, -1])

    def step(s, inp):
        dec, st = inp
        return dec[..., None, None] * s + st, s

    s0 = jnp.zeros((bsz, g, j, hd, n), jnp.float32)
    _, prev = lax.scan(step, s0, (jnp.moveaxis(chunk_decay, 1, 0), jnp.moveaxis(states, 1, 0)))
    prev = jnp.moveaxis(prev, 0, 1)
    y_off = jnp.einsum('bclgn,bcgjpn->bclgjp', cc, prev) * jnp.exp(a_cum)[..., None]
    return (y_diag + y_off).reshape(bsz, seq_len, n_heads, hd)


def _ssd_mixer(z, xbc, dt_raw, conv_w, conv_b, dt_bias, a_log, d_skip, norm_g):
    bsz, seq_len, _ = z.shape
    xbc = jax.nn.silu(_dwconv_centred(xbc, conv_w, conv_b))
    xs, bm, cm = jnp.split(xbc, [SSD_W, SSD_W + SSD_GROUPS * SSD_STATE], axis=-1)
    xh = xs.reshape(bsz, seq_len, SSD_HEADS, SSD_HD)
    bm = bm.reshape(bsz, seq_len, SSD_GROUPS, SSD_STATE)
    cm = cm.reshape(bsz, seq_len, SSD_GROUPS, SSD_STATE)
    dts = jax.nn.softplus(dt_raw.astype(jnp.float32).reshape(bsz, seq_len, 2, SSD_HEADS)
                          + dt_bias.astype(jnp.float32))
    a = -jnp.exp(a_log.astype(jnp.float32))
    y_f = _ssd_chunked(xh, dts[:, :, 0], a[0], bm, cm)
    fl = lambda t: jnp.flip(t, axis=1)
    y_b = fl(_ssd_chunked(fl(xh), fl(dts[:, :, 1]), a[1], fl(bm), fl(cm)))
    y = y_f + y_b + d_skip.astype(jnp.float32)[:, None] * xh.astype(jnp.float32)
    y = y.reshape(bsz, seq_len, SSD_W) * jax.nn.silu(z.astype(jnp.float32))
    yg = y.reshape(bsz, seq_len, SSD_GROUPS, SSD_W // SSD_GROUPS)
    yg = yg * lax.rsqrt(jnp.mean(yg * yg, axis=-1, keepdims=True) + RMS_EPS)
    return (yg.reshape(bsz, seq_len, SSD_W) * norm_g.astype(jnp.float32)).astype(z.dtype)


def _expert_choice_ffn(h, w_router, w1, w3, w2):
    bsz, n_tok, _ = h.shape
    cap = EC_CAPACITY * n_tok // N_EXPERTS
    aff = jax.nn.softmax((h @ w_router).astype(jnp.float32), axis=-1)
    gate, idx = lax.top_k(jnp.swapaxes(aff, 1, 2), cap)
    bidx = jnp.arange(bsz)[:, None, None]
    xg = h[bidx, idx]
    hid = jax.nn.silu(jnp.einsum('becd,edf->becf', xg, w1)) * jnp.einsum('becd,edf->becf', xg, w3)
    y = jnp.einsum('becf,efd->becd', hid, w2) * gate[..., None].astype(h.dtype)
    return jnp.zeros_like(h).at[bidx, idx].add(y)


def _final_norm_kernel(x_ref, g_ref, o_ref):
    x = x_ref[...]
    ms = jnp.mean(x * x, axis=-1, keepdims=True)
    o_ref[...] = x * lax.rsqrt(ms + RMS_EPS) * g_ref[...]


def _final_norm(x, g):
    bsz, seq_len, d = x.shape
    xt = x.reshape(bsz * seq_len, d)
    tm = 512
    out = pl.pallas_call(
        _final_norm_kernel,
        out_shape=jax.ShapeDtypeStruct(xt.shape, xt.dtype),
        grid=(xt.shape[0] // tm,),
        in_specs=[pl.BlockSpec((tm, d), lambda i: (i, 0)), pl.BlockSpec((1, d), lambda i: (0, 0))],
        out_specs=pl.BlockSpec((tm, d), lambda i: (i, 0)),
    )(xt, g.reshape(1, d))
    return out.reshape(bsz, seq_len, d)


def kernel(x, norm1_g, w_in, hy_conv_w, hy_conv_b, hy_fw1, hy_fb1, hy_fw2, hy_fb2, hy_fw3, hy_fb3,
           hy_fw4, hy_freq, hy_bias, lru_conv_w, lru_conv_b, lru_wa, lru_ba, lru_wx, lru_bx, lru_lam,
           da_lam, da_subln_g, ssd_conv_w, ssd_conv_b, ssd_dt_bias, ssd_a_log, ssd_d, ssd_norm_g,
           w_branch, b_gate, w_out, norm2_g, w_router, moe_w1, moe_w3, moe_w2, rel_bias, final_g):
    bsz, seq_len, d_model = x.shape
    o1 = HY_COLS
    o2 = o1 + LRU_COLS
    o3 = o2 + DA_COLS
    o4 = o3 + SSD_COLS
    for l in range(DEPTH):
        h = _rmsnorm(x, norm1_g[l])
        proj = h @ w_in[l]
        p_hy, p_lru, p_da, p_ssd, p_gate = jnp.split(proj, [o1, o2, o3, o4], axis=-1)
        o_hy = _hyena_mixer(p_hy, hy_conv_w[l], hy_conv_b[l], hy_fw1[l], hy_fb1[l], hy_fw2[l], hy_fb2[l],
                            hy_fw3[l], hy_fb3[l], hy_fw4[l], hy_freq[l], hy_bias[l])
        lru_x, lru_gate = jnp.split(p_lru, 2, axis=-1)
        o_lru = _rglru_mixer(lru_x, lru_gate, lru_conv_w[l], lru_conv_b[l], lru_wa[l], lru_ba[l],
                             lru_wx[l], lru_bx[l], lru_lam[l])
        q, k, v = jnp.split(p_da, 3, axis=-1)
        lam_init = 0.8 - 0.6 * math.exp(-0.3 * l)
        o_da = _diff_attention(q, k, v, da_lam[l], da_subln_g[l], rel_bias, lam_init)
        z, xbc, dt_raw = jnp.split(p_ssd, [SSD_W, SSD_W + SSD_XBC], axis=-1)
        o_ssd = _ssd_mixer(z, xbc, dt_raw, ssd_conv_w[l], ssd_conv_b[l], ssd_dt_bias[l], ssd_a_log[l],
                           ssd_d[l], ssd_norm_g[l])
        branches = jnp.stack([o_hy, o_lru, o_da, o_ssd], axis=2)
        gates = jax.nn.sigmoid(p_gate.reshape(bsz, seq_len, N_BRANCH, d_model) + b_gate[l])
        merged = jnp.sum(gates * jnp.einsum('blkc,kcd->blkd', branches, w_branch[l]), axis=2)
        x = x + merged @ w_out[l]
        x = x + _expert_choice_ffn(_rmsnorm(x, norm2_g[l]), w_router[l], moe_w1[l], moe_w3[l], moe_w2[l])
    return _final_norm(x, final_g)
```

```python
import math
from functools import partial

import jax
import jax.numpy as jnp
from jax import lax
from jax.experimental import pallas as pl
from jax.experimental.pallas import tpu as pltpu

D_MODEL = 2048
BATCH = 4
SEQ = 2048
DEPTH = 2

N_BRANCH = 4
BR_W = D_MODEL // 4
RMS_EPS = 1e-6

HY_W = BR_W
HY_SHORT = 3
HY_EMB = 33
HY_BANDS = (HY_EMB - 1) // 2
HY_ORDER = 64
HY_TARGET = 1e-2
HY_FAST = 0.3
HY_SLOW = 1.5

LRU_W = BR_W
LRU_HEADS = 8
LRU_HD = LRU_W // LRU_HEADS
LRU_CONV = 4
LRU_C = 8.0

DA_HEADS = 4
DA_HD = BR_W // (2 * DA_HEADS)
DA_QBLOCK = 128
REL_BUCKETS = 32
REL_MAX_DIST = 128

SSD_W = BR_W
SSD_HD = 64
SSD_HEADS = SSD_W // SSD_HD
SSD_GROUPS = 2
SSD_STATE = 128
SSD_CONV = 4
SSD_CHUNK = 128

N_EXPERTS = 16
EC_CAPACITY = 2
D_EXPERT = D_MODEL

HY_COLS = 3 * HY_W
LRU_COLS = 2 * LRU_W
DA_COLS = 3 * DA_HEADS * 2 * DA_HD
SSD_XBC = SSD_W + 2 * SSD_GROUPS * SSD_STATE
SSD_COLS = SSD_W + SSD_XBC + 2 * SSD_HEADS
GATE_COLS = N_BRANCH * D_MODEL
IN_COLS = HY_COLS + LRU_COLS + DA_COLS + SSD_COLS + GATE_COLS

V7X_LANES = 128
V7X_VMEM_BYTES = 64 * 1024 * 1024
VMEM_LIMIT_BYTES = V7X_VMEM_BYTES - 8 * 1024 * 1024

BF16 = jnp.bfloat16
F32 = jnp.float32


def _cparams(*sem):
    return pltpu.CompilerParams(dimension_semantics=sem, vmem_limit_bytes=VMEM_LIMIT_BYTES)


def _rms_rows(x, g, eps):
    ms = jnp.mean(x * x, axis=-1, keepdims=True)
    return x * lax.rsqrt(ms + eps) * g


def _inproj_kernel(x_ref, g_ref, w_ref, o_ref, h_scr):
    @pl.when(pl.program_id(1) == 0)
    def _():
        h_scr[...] = _rms_rows(x_ref[...], g_ref[...], RMS_EPS).astype(BF16)

    o_ref[...] = jnp.dot(h_scr[...], w_ref[...], preferred_element_type=F32)


def _inproj(xt, g, w_bf16, tm, tn):
    t, d = xt.shape
    n = w_bf16.shape[1]
    return pl.pallas_call(
        _inproj_kernel,
        out_shape=jax.ShapeDtypeStruct((t, n), F32),
        grid=(t // tm, n // tn),
        in_specs=[pl.BlockSpec((tm, d), lambda i, j: (i, 0)),
                  pl.BlockSpec((1, d), lambda i, j: (0, 0)),
                  pl.BlockSpec((d, tn), lambda i, j: (0, j))],
        out_specs=pl.BlockSpec((tm, tn), lambda i, j: (i, j)),
        scratch_shapes=[pltpu.VMEM((tm, d), BF16)],
        compiler_params=_cparams("parallel", "arbitrary"),
        name="inproj",
    )(xt, g, w_bf16)


def _diffattn_kernel(relb_ref, q_ref, k_ref, v_ref, bucket_ref, lamqk_ref, g_ref, o_ref, *,
                     tq, lam_init, head_dim):
    h = pl.program_id(1)
    seq = q_ref.shape[0]
    margin = REL_MAX_DIST
    bucket = bucket_ref[...]
    band = jnp.zeros(bucket.shape, F32)
    for b in range(REL_BUCKETS):
        band = jnp.where(bucket == b, relb_ref[h, b], band)
    c_neg = band[0:1, 0:1]
    c_pos = band[tq - 1:tq, tq + 2 * margin - 1:tq + 2 * margin]

    lq = lamqk_ref[...]
    lam = (jnp.exp(jnp.sum(lq[0:1] * lq[1:2], axis=-1, keepdims=True))
           - jnp.exp(jnp.sum(lq[2:3] * lq[3:4], axis=-1, keepdims=True)) + lam_init)

    lane = lax.broadcasted_iota(jnp.int32, (1, 2 * head_dim), 1)
    lo_mask = (lane < head_dim).astype(F32)
    hi_mask = 1.0 - lo_mask
    k = k_ref[...].astype(BF16)
    v = v_ref[...].astype(BF16)
    scale = head_dim ** -0.5
    dn = (((1,), (1,)), ((), ()))
    for i in range(seq // tq):
        q0 = i * tq
        q = q_ref[q0:q0 + tq, :] * scale
        q1 = (q * lo_mask).astype(BF16)
        q2 = (q * hi_mask).astype(BF16)
        lo = max(q0 - margin, 0)
        hi = min(q0 + tq + margin, seq)
        pieces = []
        if lo > 0:
            pieces.append(jnp.broadcast_to(c_neg, (tq, lo)))
        pieces.append(band[:, lo - (q0 - margin):hi - (q0 - margin)])
        if hi < seq:
            pieces.append(jnp.broadcast_to(c_pos, (tq, seq - hi)))
        bias = jnp.concatenate(pieces, axis=1) if len(pieces) > 1 else pieces[0]
        s1 = lax.dot_general(q1, k, dn, preferred_element_type=F32) + bias
        s2 = lax.dot_general(q2, k, dn, preferred_element_type=F32) + bias
        p1 = jnp.exp(s1 - jnp.max(s1, axis=-1, keepdims=True))
        p2 = jnp.exp(s2 - jnp.max(s2, axis=-1, keepdims=True))
        r1 = 1.0 / jnp.sum(p1, axis=-1, keepdims=True)
        r2 = lam / jnp.sum(p2, axis=-1, keepdims=True)
        a = p1 * r1 - p2 * r2
        o = jnp.dot(a.astype(BF16), v, preferred_element_type=F32)
        o = _rms_rows(o, g_ref[...], 1e-5) * (1.0 - lam_init)
        o_ref[q0:q0 + tq, :] = o


def _diff_attention(proj, col0, bsz, seq, lam_qk, subln_g, rel_bias, lam_init, tq=256):
    hw = 2 * DA_HD
    cb = col0 // hw
    margin = REL_MAX_DIST
    r = jnp.arange(tq)[:, None]
    c = jnp.arange(tq + 2 * margin)[None, :]
    bucket = _t5_bucket(c - margin - r)
    kern = partial(_diffattn_kernel, tq=tq, lam_init=lam_init, head_dim=DA_HD)
    return pl.pallas_call(
        kern,
        out_shape=jax.ShapeDtypeStruct((bsz * seq, DA_HEADS * hw), F32),
        grid=(bsz, DA_HEADS),
        in_specs=[pl.BlockSpec(memory_space=pltpu.SMEM),
                  pl.BlockSpec((seq, hw), lambda b, h: (b, cb + h)),
                  pl.BlockSpec((seq, hw), lambda b, h: (b, cb + DA_HEADS + h)),
                  pl.BlockSpec((seq, hw), lambda b, h: (b, cb + 2 * DA_HEADS + h)),
                  pl.BlockSpec(bucket.shape, lambda b, h: (0, 0)),
                  pl.BlockSpec(lam_qk.shape, lambda b, h: (0, 0)),
                  pl.BlockSpec((1, hw), lambda b, h: (0, 0))],
        out_specs=pl.BlockSpec((seq, hw), lambda b, h: (b, h)),
        compiler_params=_cparams("parallel", "parallel"),
        name="diffattn",
    )(rel_bias.T, proj, proj, proj, bucket, lam_qk, subln_g.reshape(1, hw))


def _merge_kernel(x_ref, g_ref, wg_ref, bg_ref, b0_ref, b1_ref, b2_ref, b3_ref, wb_ref, wo_ref,
                  o_ref, h_scr, br_scr, acc_scr):
    j = pl.program_id(1)

    @pl.when(j == 0)
    def _():
        h_scr[...] = _rms_rows(x_ref[...], g_ref[...], RMS_EPS).astype(BF16)
        for kk, b_ref in enumerate((b0_ref, b1_ref, b2_ref, b3_ref)):
            br_scr[kk] = b_ref[...].astype(BF16)
        acc_scr[...] = jnp.zeros_like(acc_scr)

    h = h_scr[...]
    m = None
    for kk in range(N_BRANCH):
        gate = jnp.dot(h, wg_ref[kk], preferred_element_type=F32) + bg_ref[kk:kk + 1, :]
        bp = jnp.dot(br_scr[kk], wb_ref[kk], preferred_element_type=F32)
        term = jax.nn.sigmoid(gate) * bp
        m = term if m is None else m + term
    acc_scr[...] += jnp.dot(m.astype(BF16), wo_ref[...], preferred_element_type=F32)

    @pl.when(j == pl.num_programs(1) - 1)
    def _():
        o_ref[...] = x_ref[...] + acc_scr[...]


def _gated_merge(xt, g, wg, bg, branches, wb, wo, tm=512, tc=256):
    t, d = xt.shape
    bw = branches[0].shape[1]
    return pl.pallas_call(
        _merge_kernel,
        out_shape=jax.ShapeDtypeStruct((t, d), F32),
        grid=(t // tm, d // tc),
        in_specs=[pl.BlockSpec((tm, d), lambda i, j: (i, 0)),
                  pl.BlockSpec((1, d), lambda i, j: (0, 0)),
                  pl.BlockSpec((N_BRANCH, d, tc), lambda i, j: (0, 0, j)),
                  pl.BlockSpec((N_BRANCH, tc), lambda i, j: (0, j))]
                 + [pl.BlockSpec((tm, bw), lambda i, j: (i, 0))] * N_BRANCH
                 + [pl.BlockSpec((N_BRANCH, bw, tc), lambda i, j: (0, 0, j)),
                    pl.BlockSpec((tc, d), lambda i, j: (j, 0))],
        out_specs=pl.BlockSpec((tm, d), lambda i, j: (i, 0)),
        scratch_shapes=[pltpu.VMEM((tm, d), BF16), pltpu.VMEM((N_BRANCH, tm, bw), BF16),
                        pltpu.VMEM((tm, d), F32)],
        compiler_params=_cparams("parallel", "arbitrary"),
        name="gated_merge",
    )(xt, g, wg, bg, *branches, wb, wo)


def _router_kernel(x_ref, g_ref, wr_ref, h_ref, aff_ref, *, n_experts):
    h = _rms_rows(x_ref[...], g_ref[...], RMS_EPS).astype(BF16)
    h_ref[...] = h
    logits = jnp.dot(h, wr_ref[...], preferred_element_type=F32)
    lane = lax.broadcasted_iota(jnp.int32, logits.shape, 1)
    logits = jnp.where(lane < n_experts, logits, -jnp.inf)
    p = jnp.exp(logits - jnp.max(logits, axis=-1, keepdims=True))
    aff_ref[...] = p / jnp.sum(p, axis=-1, keepdims=True)


def _norm_router(xt, g, w_router, tm=512):
    t, d = xt.shape
    e = w_router.shape[1]
    wr = jnp.zeros((d, V7X_LANES), BF16).at[:, :e].set(w_router.astype(BF16))
    return pl.pallas_call(
        partial(_router_kernel, n_experts=e),
        out_shape=(jax.ShapeDtypeStruct((t, d), BF16), jax.ShapeDtypeStruct((t, V7X_LANES), F32)),
        grid=(t // tm,),
        in_specs=[pl.BlockSpec((tm, d), lambda i: (i, 0)),
                  pl.BlockSpec((1, d), lambda i: (0, 0)),
                  pl.BlockSpec((d, V7X_LANES), lambda i: (0, 0))],
        out_specs=(pl.BlockSpec((tm, d), lambda i: (i, 0)),
                   pl.BlockSpec((tm, V7X_LANES), lambda i: (i, 0))),
        compiler_params=_cparams("parallel"),
        name="norm_router",
    )(xt, g, wr)


def _expert_kernel(xg_ref, w1_ref, w3_ref, w2_ref, gate_ref, o_ref, acc_scr):
    f = pl.program_id(1)

    @pl.when(f == 0)
    def _():
        acc_scr[...] = jnp.zeros_like(acc_scr)

    xg = xg_ref[0]
    a = jnp.dot(xg, w1_ref[0].astype(BF16), preferred_element_type=F32)
    b = jnp.dot(xg, w3_ref[0].astype(BF16), preferred_element_type=F32)
    hid = (a * jax.nn.sigmoid(a) * b).astype(BF16)
    acc_scr[...] += jnp.dot(hid, w2_ref[0].astype(BF16), preferred_element_type=F32)

    @pl.when(f == pl.num_programs(1) - 1)
    def _():
        o_ref[0] = acc_scr[...] * gate_ref[0]


def _experts(xg, w1, w3, w2, gate, tf=256):
    e, c, d = xg.shape
    f = w1.shape[2]
    return pl.pallas_call(
        _expert_kernel,
        out_shape=jax.ShapeDtypeStruct((e, c, d), F32),
        grid=(e, f // tf),
        in_specs=[pl.BlockSpec((1, c, d), lambda i, j: (i, 0, 0)),
                  pl.BlockSpec((1, d, tf), lambda i, j: (i, 0, j)),
                  pl.BlockSpec((1, d, tf), lambda i, j: (i, 0, j)),
                  pl.BlockSpec((1, tf, d), lambda i, j: (i, j, 0)),
                  pl.BlockSpec((1, c, 1), lambda i, j: (i, 0, 0))],
        out_specs=pl.BlockSpec((1, c, d), lambda i, j: (i, 0, 0)),
        scratch_shapes=[pltpu.VMEM((c, d), F32)],
        compiler_params=_cparams("parallel", "arbitrary"),
        name="experts",
    )(xg, w1, w3, w2, gate)


def _expert_choice_ffn(xt, bsz, n_tok, norm_g, w_router, w1, w3, w2):
    d = xt.shape[1]
    n_exp = w_router.shape[1]
    cap = EC_CAPACITY * n_tok // n_exp
    h2, aff = _norm_router(xt, norm_g.reshape(1, d), w_router)
    aff = aff[:, :n_exp].reshape(bsz, n_tok, n_exp)
    gate, idx = lax.top_k(jnp.swapaxes(aff, 1, 2), cap)
    rows = idx + (jnp.arange(bsz) * n_tok)[:, None, None]
    rows = jnp.swapaxes(rows, 0, 1).reshape(n_exp, bsz * cap)
    gate = jnp.swapaxes(gate, 0, 1).reshape(n_exp, bsz * cap, 1)
    xg = h2[rows]
    y = _experts(xg, w1, w3, w2, gate)
    return xt.at[rows.reshape(-1)].add(y.reshape(-1, d))


def _final_norm_kernel(x_ref, g_ref, o_ref):
    o_ref[...] = _rms_rows(x_ref[...], g_ref[...], RMS_EPS)


def _final_norm(xt, g, tm=512):
    t, d = xt.shape
    return pl.pallas_call(
        _final_norm_kernel,
        out_shape=jax.ShapeDtypeStruct(xt.shape, xt.dtype),
        grid=(t // tm,),
        in_specs=[pl.BlockSpec((tm, d), lambda i: (i, 0)), pl.BlockSpec((1, d), lambda i: (0, 0))],
        out_specs=pl.BlockSpec((tm, d), lambda i: (i, 0)),
        compiler_params=_cparams("parallel"),
        name="final_norm",
    )(xt, g.reshape(1, d))


def _shift_rows(x, offset):
    n = x.shape[0]
    if offset == 0:
        return x
    row = lax.broadcasted_iota(jnp.int32, x.shape, 0)
    rolled = pltpu.roll(x, (-offset) % n, axis=0)
    valid = (row + offset >= 0) & (row + offset < n)
    return jnp.where(valid, rolled, 0.0)


def _dwconv_rows(x, w_ref, b_ref):
    taps = w_ref.shape[0]
    y = b_ref[...] + jnp.zeros_like(x)
    for kk in range(taps):
        y = y + w_ref[kk:kk + 1, :] * _shift_rows(x, kk - taps // 2)
    return y


def _gelu_tanh(x):
    return 0.5 * x * (1.0 + jnp.tanh(math.sqrt(2.0 / math.pi) * (x + 0.044715 * (x * x * x))))


def _rglru_kernel(x_ref, gate_ref, cw_ref, cb_ref, wa_ref, ba_ref, wx_ref, bx_ref, sp_ref, o_ref,
                  af_scr, bf_scr, ab_scr, bb_scr):
    nb, seq, _ = x_ref.shape
    row = lax.broadcasted_iota(jnp.int32, (seq, x_ref.shape[2]), 0)
    for b in range(nb):
        xc = _dwconv_rows(x_ref[b], cw_ref, cb_ref)
        xcb = xc.astype(BF16)
        for dr, (a_scr, b_scr) in enumerate(((af_scr, bf_scr), (ab_scr, bb_scr))):
            r = jax.nn.sigmoid(jnp.dot(xcb, wa_ref[dr, 0], preferred_element_type=F32) + ba_ref[dr:dr + 1, :])
            i = jax.nn.sigmoid(jnp.dot(xcb, wx_ref[dr, 0], preferred_element_type=F32) + bx_ref[dr:dr + 1, :])
            log_a = -LRU_C * r * sp_ref[dr:dr + 1, :]
            mult = jnp.sqrt(1.0 - jnp.exp(2.0 * log_a))
            mult = jnp.where(row == (seq - 1 if dr else 0), 1.0, mult)
            a_scr[b] = jnp.exp(log_a)
            b_scr[b] = mult * i * xc

    def step(tt, carry):
        new = []
        tb = seq - 1 - tt
        for b in range(nb):
            hf, hb = carry[b]
            hf = af_scr[b, pl.ds(tt, 1), :] * hf + bf_scr[b, pl.ds(tt, 1), :]
            hb = ab_scr[b, pl.ds(tb, 1), :] * hb + bb_scr[b, pl.ds(tb, 1), :]
            bf_scr[b, pl.ds(tt, 1), :] = hf
            bb_scr[b, pl.ds(tb, 1), :] = hb
            new.append((hf, hb))
        return tuple(new)

    zero = jnp.zeros((1, x_ref.shape[2]), F32)
    lax.fori_loop(0, seq, step, tuple((zero, zero) for _ in range(nb)), unroll=8)
    for b in range(nb):
        o_ref[b] = (bf_scr[b] + bb_scr[b]) * _gelu_tanh(gate_ref[b])


def _block_diag_tiles(w, tile):
    nd, nh, hd, _ = w.shape
    per = tile // hd
    w = w.reshape(nd, nh // per, per, hd, hd)
    eye = jnp.eye(per, dtype=w.dtype)
    bd = jnp.einsum('dgpij,pq->dgpiqj', w, eye).reshape(nd, nh // per, tile, tile)
    return bd.astype(BF16)


def _rglru_mixer(proj3, col_x, col_gate, conv_w, conv_b, w_a, b_a, w_x, b_x, lam):
    bsz, seq, _ = proj3.shape
    tc = V7X_LANES
    nct = LRU_W // tc
    sp = jax.nn.softplus(-lam.astype(F32))
    wa = _block_diag_tiles(w_a, tc)
    wx = _block_diag_tiles(w_x, tc)
    vec = lambda i: (0, i)
    return pl.pallas_call(
        _rglru_kernel,
        out_shape=jax.ShapeDtypeStruct((bsz, seq, LRU_W), F32),
        grid=(nct,),
        in_specs=[pl.BlockSpec((bsz, seq, tc), lambda i: (0, 0, col_x // tc + i)),
                  pl.BlockSpec((bsz, seq, tc), lambda i: (0, 0, col_gate // tc + i)),
                  pl.BlockSpec((LRU_CONV, tc), vec),
                  pl.BlockSpec((1, tc), vec),
                  pl.BlockSpec((2, 1, tc, tc), lambda i: (0, i, 0, 0)),
                  pl.BlockSpec((2, tc), vec),
                  pl.BlockSpec((2, 1, tc, tc), lambda i: (0, i, 0, 0)),
                  pl.BlockSpec((2, tc), vec),
                  pl.BlockSpec((2, tc), vec)],
        out_specs=pl.BlockSpec((bsz, seq, tc), lambda i: (0, 0, i)),
        scratch_shapes=[pltpu.VMEM((bsz, seq, tc), F32)] * 4,
        compiler_params=_cparams("parallel"),
        name="rglru",
    )(proj3, proj3, conv_w, conv_b.reshape(1, -1), wa, b_a, wx, b_x, sp)


def _dwconv_centred(x, w, b):
    k = w.shape[0]
    left = k // 2
    right = k - 1 - left
    y = lax.conv_general_dilated(x, w[:, None, :].astype(x.dtype), window_strides=(1,),
                                 padding=[(left, right)], dimension_numbers=('NWC', 'WIO', 'NWC'),
                                 feature_group_count=x.shape[-1])
    return y + b.astype(x.dtype)


def _hyena_filter(seq_len, fw1, fb1, fw2, fb2, fw3, fb3, fw4, freq):
    f32 = jnp.float32
    t = jnp.linspace(0.0, 1.0, seq_len, dtype=f32)[:, None]
    w = (2.0 * math.pi / seq_len) * jnp.arange(seq_len, dtype=f32)[:, None]
    bands = jnp.linspace(1e-4, HY_BANDS - 1, HY_BANDS, dtype=f32)[None, :]
    z = jnp.concatenate([t, jnp.cos(bands * w), -jnp.sin(bands * w)], axis=-1).astype(fw1.dtype)
    hid = jnp.sin(freq * (z @ fw1 + fb1))
    hid = jnp.sin(freq * (hid @ fw2 + fb2))
    hid = jnp.sin(freq * (hid @ fw3 + fb3))
    filt = (hid @ fw4).astype(f32)
    deltas = jnp.abs(jnp.linspace(math.log(HY_TARGET) / HY_SLOW, math.log(HY_TARGET) / HY_FAST,
                                  HY_W, dtype=f32))
    filt = filt * jnp.exp(-2.0 * jnp.abs(t - 0.5) * deltas[None, :])
    return filt / jnp.sum(jnp.abs(filt), axis=0, keepdims=True)


def _hyena_mixer(u, conv_w, conv_b, fw1, fb1, fw2, fb2, fw3, fb3, fw4, freq, fft_bias):
    _, seq_len, _ = u.shape
    x0, x1, v = jnp.split(_dwconv_centred(u, conv_w, conv_b), 3, axis=-1)
    v = (v * x1).astype(jnp.float32)
    filt = _hyena_filter(seq_len, fw1, fb1, fw2, fb2, fw3, fb3, fw4, freq)
    n_fft = 2 * seq_len
    y = jnp.fft.irfft(jnp.fft.rfft(v, n=n_fft, axis=1) * jnp.fft.rfft(filt, n=n_fft, axis=0)[None],
                      n=n_fft, axis=1)
    start = seq_len // 2
    y = y[:, start:start + seq_len] + v * fft_bias.astype(jnp.float32)
    return y.astype(u.dtype) * x0


def _t5_bucket(rel):
    nb = REL_BUCKETS // 2
    ret = (rel > 0).astype(jnp.int32) * nb
    n = jnp.abs(rel)
    max_exact = nb // 2
    large = max_exact + (jnp.log(jnp.maximum(n, 1).astype(jnp.float32) / max_exact)
                         / math.log(REL_MAX_DIST / max_exact) * (nb - max_exact)).astype(jnp.int32)
    large = jnp.minimum(large, nb - 1)
    return ret + jnp.where(n < max_exact, n, large)


def _ssd_chunked(x, dt, a, bm, cm):
    bsz, seq_len, n_heads, hd = x.shape
    g = bm.shape[2]
    j = n_heads // g
    n = bm.shape[-1]
    q = SSD_CHUNK
    c = seq_len // q
    xd = (x.astype(jnp.float32) * dt[..., None]).reshape(bsz, c, q, g, j, hd)
    a_cum = jnp.cumsum((dt * a).reshape(bsz, c, q, g, j), axis=2)
    bc = bm.astype(jnp.float32).reshape(bsz, c, q, g, n)
    cc = cm.astype(jnp.float32).reshape(bsz, c, q, g, n)
    seg = a_cum[:, :, :, None] - a_cum[:, :, None, :]
    lower = jnp.tril(jnp.ones((q, q), dtype=bool))[:, :, None, None]
    l_mat = jnp.exp(jnp.where(lower, seg, -jnp.inf))
    cb = jnp.einsum('bclgn,bcsgn->bclsg', cc, bc)
    y_diag = jnp.einsum('bclsgj,bcsgjp->bclgjp', cb[..., None] * l_mat, xd)
    decay_s = jnp.exp(a_cum[:, :, -1:] - a_cum)
    states = jnp.einsum('bclgn,bclgjp->bcgjpn', bc, xd * decay_s[..., None])
    chunk_decay = jnp.exp(a_cum[:, :, -1])

    def step(s, inp):
        dec, st = inp
        return dec[..., None, None] * s + st, s

    s0 = jnp.zeros((bsz, g, j, hd, n), jnp.float32)
    _, prev = lax.scan(step, s0, (jnp.moveaxis(chunk_decay, 1, 0), jnp.moveaxis(states, 1, 0)))
    prev = jnp.moveaxis(prev, 0, 1)
    y_off = jnp.einsum('bclgn,bcgjpn->bclgjp', cc, prev) * jnp.exp(a_cum)[..., None]
    return (y_diag + y_off).reshape(bsz, seq_len, n_heads, hd)


def _ssd_mixer(z, xbc, dt_raw, conv_w, conv_b, dt_bias, a_log, d_skip, norm_g):
    bsz, seq_len, _ = z.shape
    xbc = jax.nn.silu(_dwconv_centred(xbc, conv_w, conv_b))
    xs, bm, cm = jnp.split(xbc, [SSD_W, SSD_W + SSD_GROUPS * SSD_STATE], axis=-1)
    xh = xs.reshape(bsz, seq_len, SSD_HEADS, SSD_HD)
    bm = bm.reshape(bsz, seq_len, SSD_GROUPS, SSD_STATE)
    cm = cm.reshape(bsz, seq_len, SSD_GROUPS, SSD_STATE)
    dts = jax.nn.softplus(dt_raw.astype(jnp.float32).reshape(bsz, seq_len, 2, SSD_HEADS)
                          + dt_bias.astype(jnp.float32))
    a = -jnp.exp(a_log.astype(jnp.float32))
    y_f = _ssd_chunked(xh, dts[:, :, 0], a[0], bm, cm)
    fl = lambda t: jnp.flip(t, axis=1)
    y_b = fl(_ssd_chunked(fl(xh), fl(dts[:, :, 1]), a[1], fl(bm), fl(cm)))
    y = y_f + y_b + d_skip.astype(jnp.float32)[:, None] * xh.astype(jnp.float32)
    y = y.reshape(bsz, seq_len, SSD_W) * jax.nn.silu(z.astype(jnp.float32))
    yg = y.reshape(bsz, seq_len, SSD_GROUPS, SSD_W // SSD_GROUPS)
    yg = yg * lax.rsqrt(jnp.mean(yg * yg, axis=-1, keepdims=True) + RMS_EPS)
    return (yg.reshape(bsz, seq_len, SSD_W) * norm_g.astype(jnp.float32)).astype(z.dtype)


MIX_COLS = HY_COLS + LRU_COLS + DA_COLS + SSD_W + SSD_XBC
DT_COLS = 2 * SSD_HEADS
MIX_PAD = MIX_COLS + V7X_LANES


def kernel(x, norm1_g, w_in, hy_conv_w, hy_conv_b, hy_fw1, hy_fb1, hy_fw2, hy_fb2, hy_fw3, hy_fb3,
           hy_fw4, hy_freq, hy_bias, lru_conv_w, lru_conv_b, lru_wa, lru_ba, lru_wx, lru_bx, lru_lam,
           da_lam, da_subln_g, ssd_conv_w, ssd_conv_b, ssd_dt_bias, ssd_a_log, ssd_d, ssd_norm_g,
           w_branch, b_gate, w_out, norm2_g, w_router, moe_w1, moe_w3, moe_w2, rel_bias, final_g):
    bsz, seq_len, d_model = x.shape
    t = bsz * seq_len
    xt = x.reshape(t, d_model)
    o1 = HY_COLS
    o2 = o1 + LRU_COLS
    o3 = o2 + DA_COLS
    o4 = o3 + SSD_W + SSD_XBC
    for l in range(DEPTH):
        g1 = norm1_g[l].reshape(1, d_model)
        w_mix = jnp.pad(w_in[l][:, :MIX_COLS + DT_COLS].astype(BF16),
                        ((0, 0), (0, V7X_LANES - DT_COLS)))
        proj = _inproj(xt, g1, w_mix, tm=1024, tn=MIX_PAD // 9)
        p3 = proj.reshape(bsz, seq_len, MIX_PAD)
        o_hy = _hyena_mixer(p3[..., :o1], hy_conv_w[l], hy_conv_b[l], hy_fw1[l], hy_fb1[l], hy_fw2[l],
                            hy_fb2[l], hy_fw3[l], hy_fb3[l], hy_fw4[l], hy_freq[l], hy_bias[l])
        o_lru = _rglru_mixer(p3, o1, o1 + LRU_W, lru_conv_w[l], lru_conv_b[l],
                             lru_wa[l], lru_ba[l], lru_wx[l], lru_bx[l], lru_lam[l])
        lam_init = 0.8 - 0.6 * math.exp(-0.3 * l)
        o_da = _diff_attention(proj, o2, bsz, seq_len, da_lam[l], da_subln_g[l], rel_bias, lam_init)
        o_ssd = _ssd_mixer(p3[..., o3:o3 + SSD_W], p3[..., o3 + SSD_W:o4], p3[..., o4:o4 + DT_COLS],
                           ssd_conv_w[l], ssd_conv_b[l], ssd_dt_bias[l], ssd_a_log[l], ssd_d[l],
                           ssd_norm_g[l])
        wg = jnp.transpose(w_in[l][:, MIX_COLS + DT_COLS:].astype(BF16).reshape(d_model, N_BRANCH, d_model),
                           (1, 0, 2))
        branches = [o_hy.reshape(t, BR_W), o_lru.reshape(t, BR_W), o_da, o_ssd.reshape(t, BR_W)]
        xt = _gated_merge(xt, g1, wg, b_gate[l], branches, w_branch[l].astype(BF16),
                          w_out[l].astype(BF16))
        xt = _expert_choice_ffn(xt, bsz, seq_len, norm2_g[l], w_router[l], moe_w1[l], moe_w3[l], moe_w2[l])
    return _final_norm(xt, final_g).reshape(bsz, seq_len, d_model)
```

```python
import math
from functools import partial

import numpy as np
import jax
import jax.numpy as jnp
from jax import lax
from jax.experimental import pallas as pl
from jax.experimental.pallas import tpu as pltpu

D_MODEL = 2048
BATCH = 4
SEQ = 2048
DEPTH = 2

N_BRANCH = 4
BR_W = D_MODEL // 4
RMS_EPS = 1e-6

HY_W = BR_W
HY_SHORT = 3
HY_EMB = 33
HY_BANDS = (HY_EMB - 1) // 2
HY_ORDER = 64
HY_TARGET = 1e-2
HY_FAST = 0.3
HY_SLOW = 1.5

LRU_W = BR_W
LRU_HEADS = 8
LRU_HD = LRU_W // LRU_HEADS
LRU_CONV = 4
LRU_C = 8.0

DA_HEADS = 4
DA_HD = BR_W // (2 * DA_HEADS)
DA_QBLOCK = 128
REL_BUCKETS = 32
REL_MAX_DIST = 128

SSD_W = BR_W
SSD_HD = 64
SSD_HEADS = SSD_W // SSD_HD
SSD_GROUPS = 2
SSD_STATE = 128
SSD_CONV = 4
SSD_CHUNK = 128

N_EXPERTS = 16
EC_CAPACITY = 2
D_EXPERT = D_MODEL

HY_COLS = 3 * HY_W
LRU_COLS = 2 * LRU_W
DA_COLS = 3 * DA_HEADS * 2 * DA_HD
SSD_XBC = SSD_W + 2 * SSD_GROUPS * SSD_STATE
SSD_COLS = SSD_W + SSD_XBC + 2 * SSD_HEADS
GATE_COLS = N_BRANCH * D_MODEL
IN_COLS = HY_COLS + LRU_COLS + DA_COLS + SSD_COLS + GATE_COLS

V7X_LANES = 128
V7X_VMEM_BYTES = 64 * 1024 * 1024
VMEM_LIMIT_BYTES = V7X_VMEM_BYTES - 8 * 1024 * 1024

BF16 = jnp.bfloat16
F32 = jnp.float32


def _cparams(*sem):
    return pltpu.CompilerParams(dimension_semantics=sem, vmem_limit_bytes=VMEM_LIMIT_BYTES)


def _rms_rows(x, g, eps):
    ms = jnp.mean(x * x, axis=-1, keepdims=True)
    return x * lax.rsqrt(ms + eps) * g


def _inproj_kernel(x_ref, g_ref, w_ref, ws_ref, o_ref, os_ref, h_scr):
    @pl.when(pl.program_id(1) == 0)
    def _():
        h_scr[...] = _rms_rows(x_ref[...], g_ref[...], RMS_EPS).astype(BF16)
        os_ref[...] = jnp.dot(h_scr[...], ws_ref[...], preferred_element_type=F32)

    o_ref[...] = jnp.dot(h_scr[...], w_ref[...], preferred_element_type=F32)


def _inproj(xt, g, w_bf16, w_small, tm=1024, tn=512):
    t, d = xt.shape
    n = w_bf16.shape[1]
    ns = w_small.shape[1]
    return pl.pallas_call(
        _inproj_kernel,
        out_shape=(jax.ShapeDtypeStruct((t, n), F32), jax.ShapeDtypeStruct((t, ns), F32)),
        grid=(t // tm, n // tn),
        in_specs=[pl.BlockSpec((tm, d), lambda i, j: (i, 0)),
                  pl.BlockSpec((1, d), lambda i, j: (0, 0)),
                  pl.BlockSpec((d, tn), lambda i, j: (0, j)),
                  pl.BlockSpec((d, ns), lambda i, j: (0, 0))],
        out_specs=(pl.BlockSpec((tm, tn), lambda i, j: (i, j)), pl.BlockSpec((tm, ns), lambda i, j: (i, 0))),
        scratch_shapes=[pltpu.VMEM((tm, d), BF16)],
        compiler_params=_cparams("parallel", "arbitrary"),
        name="inproj",
    )(xt, g, w_bf16, w_small)


def _diffattn_kernel(relb_ref, q_ref, k_ref, v_ref, bucket_ref, lamqk_ref, g_ref, o_ref, *,
                     tq, lam_init, head_dim):
    h = pl.program_id(1)
    seq = q_ref.shape[0]
    margin = REL_MAX_DIST
    bucket = bucket_ref[...]
    band = jnp.zeros(bucket.shape, F32)
    for b in range(REL_BUCKETS):
        band = jnp.where(bucket == b, relb_ref[h, b], band)
    c_neg = band[0:1, 0:1]
    c_pos = band[tq - 1:tq, tq + 2 * margin - 1:tq + 2 * margin]

    lq = lamqk_ref[...]
    lam = (jnp.exp(jnp.sum(lq[0:1] * lq[1:2], axis=-1, keepdims=True))
           - jnp.exp(jnp.sum(lq[2:3] * lq[3:4], axis=-1, keepdims=True)) + lam_init)

    lane = lax.broadcasted_iota(jnp.int32, (1, 2 * head_dim), 1)
    lo_mask = (lane < head_dim).astype(F32)
    hi_mask = 1.0 - lo_mask
    k = k_ref[...].astype(BF16)
    v = v_ref[...].astype(BF16)
    scale = head_dim ** -0.5
    dn = (((1,), (1,)), ((), ()))
    for i in range(seq // tq):
        q0 = i * tq
        q = q_ref[q0:q0 + tq, :] * scale
        q1 = (q * lo_mask).astype(BF16)
        q2 = (q * hi_mask).astype(BF16)
        lo = max(q0 - margin, 0)
        hi = min(q0 + tq + margin, seq)
        pieces = []
        if lo > 0:
            pieces.append(jnp.broadcast_to(c_neg, (tq, lo)))
        pieces.append(band[:, lo - (q0 - margin):hi - (q0 - margin)])
        if hi < seq:
            pieces.append(jnp.broadcast_to(c_pos, (tq, seq - hi)))
        bias = jnp.concatenate(pieces, axis=1) if len(pieces) > 1 else pieces[0]
        s1 = lax.dot_general(q1, k, dn, preferred_element_type=F32) + bias
        s2 = lax.dot_general(q2, k, dn, preferred_element_type=F32) + bias
        p1 = jnp.exp(s1 - jnp.max(s1, axis=-1, keepdims=True))
        p2 = jnp.exp(s2 - jnp.max(s2, axis=-1, keepdims=True))
        r1 = 1.0 / jnp.sum(p1, axis=-1, keepdims=True)
        r2 = lam / jnp.sum(p2, axis=-1, keepdims=True)
        a = p1 * r1 - p2 * r2
        o = jnp.dot(a.astype(BF16), v, preferred_element_type=F32)
        o = _rms_rows(o, g_ref[...], 1e-5) * (1.0 - lam_init)
        o_ref[q0:q0 + tq, :] = o


def _diff_attention(proj, col0, bsz, seq, lam_qk, subln_g, rel_bias, lam_init, tq=256):
    hw = 2 * DA_HD
    cb = col0 // hw
    margin = REL_MAX_DIST
    r = jnp.arange(tq)[:, None]
    c = jnp.arange(tq + 2 * margin)[None, :]
    bucket = _t5_bucket(c - margin - r)
    kern = partial(_diffattn_kernel, tq=tq, lam_init=lam_init, head_dim=DA_HD)
    return pl.pallas_call(
        kern,
        out_shape=jax.ShapeDtypeStruct((bsz * seq, DA_HEADS * hw), F32),
        grid=(bsz, DA_HEADS),
        in_specs=[pl.BlockSpec(memory_space=pltpu.SMEM),
                  pl.BlockSpec((seq, hw), lambda b, h: (b, cb + h)),
                  pl.BlockSpec((seq, hw), lambda b, h: (b, cb + DA_HEADS + h)),
                  pl.BlockSpec((seq, hw), lambda b, h: (b, cb + 2 * DA_HEADS + h)),
                  pl.BlockSpec(bucket.shape, lambda b, h: (0, 0)),
                  pl.BlockSpec(lam_qk.shape, lambda b, h: (0, 0)),
                  pl.BlockSpec((1, hw), lambda b, h: (0, 0))],
        out_specs=pl.BlockSpec((seq, hw), lambda b, h: (b, h)),
        compiler_params=_cparams("parallel", "parallel"),
        name="diffattn",
    )(rel_bias.T, proj, proj, proj, bucket, lam_qk, subln_g.reshape(1, hw))


def _merge_kernel(x_ref, g_ref, wg0_ref, wg1_ref, wg2_ref, wg3_ref, bg_ref, b0_ref, b1_ref, b2_ref, b3_ref,
                  wb_ref, wo_ref, o_ref, h_scr, br_scr, acc_scr):
    j = pl.program_id(1)
    wg_ref = (wg0_ref, wg1_ref, wg2_ref, wg3_ref)

    @pl.when(j == 0)
    def _():
        h_scr[...] = _rms_rows(x_ref[...], g_ref[...], RMS_EPS).astype(BF16)
        for kk, b_ref in enumerate((b0_ref, b1_ref, b2_ref, b3_ref)):
            br_scr[kk] = b_ref[...].astype(BF16)
        acc_scr[...] = jnp.zeros_like(acc_scr)

    h = h_scr[...]
    m = None
    for kk in range(N_BRANCH):
        gate = jnp.dot(h, wg_ref[kk][...], preferred_element_type=F32) + bg_ref[kk:kk + 1, :]
        bp = jnp.dot(br_scr[kk], wb_ref[kk], preferred_element_type=F32)
        term = jax.nn.sigmoid(gate) * bp
        m = term if m is None else m + term
    acc_scr[...] += jnp.dot(m.astype(BF16), wo_ref[...], preferred_element_type=F32)

    @pl.when(j == pl.num_programs(1) - 1)
    def _():
        o_ref[...] = x_ref[...] + acc_scr[...]


def _gated_merge(xt, g, wg, bg, branches, wb, wo, tm=512, tc=256):
    t, d = xt.shape
    bw = branches[0].shape[1]
    ncb = d // tc
    return pl.pallas_call(
        _merge_kernel,
        out_shape=jax.ShapeDtypeStruct((t, d), F32),
        grid=(t // tm, ncb),
        in_specs=[pl.BlockSpec((tm, d), lambda i, j: (i, 0)),
                  pl.BlockSpec((1, d), lambda i, j: (0, 0))]
                 + [pl.BlockSpec((d, tc), lambda i, j, kk=kk: (0, kk * ncb + j)) for kk in range(N_BRANCH)]
                 + [pl.BlockSpec((N_BRANCH, tc), lambda i, j: (0, j))]
                 + [pl.BlockSpec((tm, bw), lambda i, j: (i, 0))] * N_BRANCH
                 + [pl.BlockSpec((N_BRANCH, bw, tc), lambda i, j: (0, 0, j)),
                    pl.BlockSpec((tc, d), lambda i, j: (j, 0))],
        out_specs=pl.BlockSpec((tm, d), lambda i, j: (i, 0)),
        scratch_shapes=[pltpu.VMEM((tm, d), BF16), pltpu.VMEM((N_BRANCH, tm, bw), BF16),
                        pltpu.VMEM((tm, d), F32)],
        compiler_params=_cparams("parallel", "arbitrary"),
        name="gated_merge",
    )(xt, g, wg, wg, wg, wg, bg, *branches, wb, wo)


def _pack_bf16_halves(h):
    half = h.shape[1] // 2
    bits = lax.bitcast_convert_type(h.astype(F32), jnp.uint32)
    return (bits[:, :half] >> 16) | (bits[:, half:] & jnp.uint32(0xFFFF0000))


def _unpack_bf16_halves(p):
    lo = lax.bitcast_convert_type(p << 16, F32).astype(BF16)
    hi = lax.bitcast_convert_type(p & jnp.uint32(0xFFFF0000), F32).astype(BF16)
    return lo, hi


def _router_kernel(x_ref, g_ref, wr_ref, h_ref, aff_ref, *, n_experts):
    h = _rms_rows(x_ref[...], g_ref[...], RMS_EPS).astype(BF16)
    h_ref[...] = _pack_bf16_halves(h)
    logits = jnp.dot(h, wr_ref[...], preferred_element_type=F32)
    lane = lax.broadcasted_iota(jnp.int32, logits.shape, 1)
    logits = jnp.where(lane < n_experts, logits, -jnp.inf)
    p = jnp.exp(logits - jnp.max(logits, axis=-1, keepdims=True))
    aff_ref[...] = p / jnp.sum(p, axis=-1, keepdims=True)


def _norm_router(xt, g, w_router, tm=512):
    t, d = xt.shape
    e = w_router.shape[1]
    wr = jnp.zeros((d, V7X_LANES), BF16).at[:, :e].set(w_router.astype(BF16))
    return pl.pallas_call(
        partial(_router_kernel, n_experts=e),
        out_shape=(jax.ShapeDtypeStruct((t, d // 2), jnp.uint32), jax.ShapeDtypeStruct((t, V7X_LANES), F32)),
        grid=(t // tm,),
        in_specs=[pl.BlockSpec((tm, d), lambda i: (i, 0)),
                  pl.BlockSpec((1, d), lambda i: (0, 0)),
                  pl.BlockSpec((d, V7X_LANES), lambda i: (0, 0))],
        out_specs=(pl.BlockSpec((tm, d // 2), lambda i: (i, 0)),
                   pl.BlockSpec((tm, V7X_LANES), lambda i: (i, 0))),
        compiler_params=_cparams("parallel"),
        name="norm_router",
    )(xt, g, wr)


def _expert_kernel(xg_ref, w1_ref, w3_ref, w2_ref, gate_ref, o_ref, xlo_scr, xhi_scr, acc_scr):
    f = pl.program_id(1)
    half = xlo_scr.shape[1]

    @pl.when(f == 0)
    def _():
        lanes = xg_ref.shape[3]
        for s in range(xg_ref.shape[2]):
            lo, hi = _unpack_bf16_halves(xg_ref[0, :, s, :])
            xlo_scr[:, s * lanes:(s + 1) * lanes] = lo
            xhi_scr[:, s * lanes:(s + 1) * lanes] = hi
        acc_scr[...] = jnp.zeros_like(acc_scr)

    xlo = xlo_scr[...]
    xhi = xhi_scr[...]

    def up(w_ref):
        return (jnp.dot(xlo, w_ref[0, :half, :].astype(BF16), preferred_element_type=F32)
                + jnp.dot(xhi, w_ref[0, half:, :].astype(BF16), preferred_element_type=F32))

    a = up(w1_ref)
    b = up(w3_ref)
    hid = (a * jax.nn.sigmoid(a) * b).astype(BF16)
    acc_scr[...] += jnp.dot(hid, w2_ref[0].astype(BF16), preferred_element_type=F32)

    @pl.when(f == pl.num_programs(1) - 1)
    def _():
        o_ref[0] = (acc_scr[...] * gate_ref[0]).astype(BF16)


def _experts(xg, w1, w3, w2, gate, tf=256):
    e, c, sub, lanes = xg.shape
    half = sub * lanes
    d = 2 * half
    f = w1.shape[2]
    return pl.pallas_call(
        _expert_kernel,
        out_shape=jax.ShapeDtypeStruct((e, c, d), BF16),
        grid=(e, f // tf),
        in_specs=[pl.BlockSpec((1, c, sub, lanes), lambda i, j: (i, 0, 0, 0)),
                  pl.BlockSpec((1, d, tf), lambda i, j: (i, 0, j)),
                  pl.BlockSpec((1, d, tf), lambda i, j: (i, 0, j)),
                  pl.BlockSpec((1, tf, d), lambda i, j: (i, j, 0)),
                  pl.BlockSpec((1, c, 1), lambda i, j: (i, 0, 0))],
        out_specs=pl.BlockSpec((1, c, d), lambda i, j: (i, 0, 0)),
        scratch_shapes=[pltpu.VMEM((c, half), BF16), pltpu.VMEM((c, half), BF16), pltpu.VMEM((c, d), F32)],
        compiler_params=_cparams("parallel", "arbitrary"),
        name="experts",
    )(xg, w1, w3, w2, gate)


def _prefix_count(mask_f, tri):
    rows, n = mask_f.shape
    w = tri.shape[0]
    run = jnp.zeros((rows, 1), F32)
    outs = []
    for c in range(n // w):
        blk = mask_f[:, c * w:(c + 1) * w]
        outs.append(jnp.dot(blk.astype(BF16), tri, preferred_element_type=F32) + run)
        run = run + jnp.sum(blk, axis=-1, keepdims=True)
    return jnp.concatenate(outs, axis=1), run


def _topk_kernel(aff_ref, idx_ref, gate_ref, *, cap):
    aff = aff_ref[0]
    n_exp, n_tok = aff.shape
    keys = lax.bitcast_convert_type(aff, jnp.int32)
    thr = jnp.zeros((n_exp, 1), jnp.int32)
    for bit in range(30, -1, -1):
        cand = thr | (1 << bit)
        cnt = jnp.sum((keys >= cand).astype(F32), axis=-1, keepdims=True)
        thr = jnp.where(cnt >= cap, cand, thr)
    gt = (keys > thr).astype(F32)
    eq = (keys == thr).astype(F32)
    w = V7X_LANES
    r_i = lax.broadcasted_iota(jnp.int32, (w, w), 0)
    c_i = lax.broadcasted_iota(jnp.int32, (w, w), 1)
    tri = (r_i < c_i).astype(BF16)
    need = cap - jnp.sum(gt, axis=-1, keepdims=True)
    eq_rank, _ = _prefix_count(eq, tri)
    sel = gt + eq * (eq_rank < need).astype(F32)
    pos, _ = _prefix_count(sel, tri)
    tok = lax.broadcasted_iota(jnp.int32, (1, n_tok), 1)
    a_h = aff.astype(BF16)
    rem = aff - a_h.astype(F32)
    a_m = rem.astype(BF16)
    a_l = (rem - a_m.astype(F32)).astype(BF16)
    slot = lax.broadcasted_iota(jnp.int32, (cap, n_tok), 0).astype(F32)
    pos = jnp.where(sel > 0.5, pos, -1.0)
    dn = (((1,), (1,)), ((), ()))
    for e in range(n_exp):
        hit = jnp.where(pos[e:e + 1, :] == slot, 1.0, 0.0).astype(BF16)
        src = jnp.concatenate([(tok >> 6).astype(F32), (tok & 63).astype(F32),
                               a_h[e:e + 1, :].astype(F32), a_m[e:e + 1, :].astype(F32),
                               a_l[e:e + 1, :].astype(F32), jnp.zeros((3, n_tok), F32)], axis=0).astype(BF16)
        res = lax.dot_general(src, hit, dn, preferred_element_type=F32)
        idx_ref[0, e:e + 1, :] = (res[0:1] * 64.0 + res[1:2]).astype(jnp.int32)
        gate_ref[0, e:e + 1, :] = res[2:3] + res[3:4] + res[4:5]


def _topk_select(aff_t, cap):
    bsz, n_exp, n_tok = aff_t.shape
    blk = lambda n: pl.BlockSpec((1, n_exp, n), lambda b: (b, 0, 0))
    return pl.pallas_call(
        partial(_topk_kernel, cap=cap),
        out_shape=(jax.ShapeDtypeStruct((bsz, n_exp, cap), jnp.int32),
                   jax.ShapeDtypeStruct((bsz, n_exp, cap), F32)),
        grid=(bsz,),
        in_specs=[blk(n_tok)],
        out_specs=(blk(cap), blk(cap)),
        compiler_params=_cparams("parallel"),
        name="topk_select",
    )(aff_t)


def _gather_kernel(rows_ref, h_hbm, o_ref, sem):
    e = pl.program_id(0)
    n = o_ref.shape[1]

    def row_copy(j):
        return pltpu.make_async_copy(h_hbm.at[rows_ref[e, j]], o_ref.at[0, j], sem)

    def start(j, carry):
        row_copy(j).start()
        return carry

    def wait(j, carry):
        row_copy(j).wait()
        return carry

    lax.fori_loop(0, n, start, 0)
    lax.fori_loop(0, n, wait, 0)


def _moe_gather(h, rows):
    n_exp, c = rows.shape
    _, sub, lanes = h.shape
    return pl.pallas_call(
        _gather_kernel,
        out_shape=jax.ShapeDtypeStruct((n_exp, c, sub, lanes), h.dtype),
        grid_spec=pltpu.PrefetchScalarGridSpec(
            num_scalar_prefetch=1,
            grid=(n_exp,),
            in_specs=[pl.BlockSpec(memory_space=pl.ANY)],
            out_specs=pl.BlockSpec((1, c, sub, lanes), lambda e, rows: (e, 0, 0, 0)),
            scratch_shapes=[pltpu.SemaphoreType.DMA(())]),
        compiler_params=_cparams("arbitrary"),
        name="moe_gather",
    )(rows, h)


def _combine_kernel(idx_ref, x_ref, y_ref, o_ref):
    tq = x_ref.shape[0]
    q0 = pl.program_id(2) * tq
    n_exp, cap, td = y_ref.shape
    tokens = q0 + lax.broadcasted_iota(jnp.int32, (tq, 1), 0)
    hit = jnp.where(idx_ref[0] == tokens, 1.0, 0.0).astype(BF16)
    y = y_ref[...].reshape(n_exp * cap, td)
    o_ref[...] = x_ref[...] + jnp.dot(hit, y, preferred_element_type=F32)


def _moe_combine(xt, y, idx, bsz, tq=512, td=512):
    t, d = xt.shape
    n_tok = t // bsz
    n_exp, _, cap = idx.shape[1], None, idx.shape[2]
    idx_flat = idx.reshape(bsz, 1, n_exp * cap)
    tq = min(tq, n_tok)
    td = min(td, d)
    nq = n_tok // tq
    return pl.pallas_call(
        _combine_kernel,
        out_shape=jax.ShapeDtypeStruct((t, d), F32),
        grid=(bsz, d // td, nq),
        in_specs=[pl.BlockSpec((1, 1, n_exp * cap), lambda b, j, q: (b, 0, 0)),
                  pl.BlockSpec((tq, td), lambda b, j, q: (b * nq + q, j)),
                  pl.BlockSpec((n_exp, cap, td), lambda b, j, q: (0, b, j))],
        out_specs=pl.BlockSpec((tq, td), lambda b, j, q: (b * nq + q, j)),
        compiler_params=_cparams("parallel", "parallel", "arbitrary"),
        name="moe_combine",
    )(idx_flat, xt, y)


def _expert_choice_ffn(xt, bsz, n_tok, norm_g, w_router, w1, w3, w2):
    d = xt.shape[1]
    n_exp = w_router.shape[1]
    cap = EC_CAPACITY * n_tok // n_exp
    h2, aff = _norm_router(xt, norm_g.reshape(1, d), w_router)
    aff_t = jnp.swapaxes(aff[:, :n_exp].reshape(bsz, n_tok, n_exp), 1, 2)
    idx, gate = _topk_select(aff_t, cap)
    rows = idx + (jnp.arange(bsz, dtype=jnp.int32) * n_tok)[:, None, None]
    rows = jnp.swapaxes(rows, 0, 1).reshape(n_exp, bsz * cap)
    gate = jnp.swapaxes(gate, 0, 1).reshape(n_exp, bsz * cap, 1)
    xg = _moe_gather(h2.reshape(h2.shape[0], -1, V7X_LANES), rows)
    y = _experts(xg, w1, w3, w2, gate)
    return _moe_combine(xt, y, idx, bsz)


def _final_norm_kernel(x_ref, g_ref, o_ref):
    o_ref[...] = _rms_rows(x_ref[...], g_ref[...], RMS_EPS)


def _final_norm(xt, g, tm=512):
    t, d = xt.shape
    return pl.pallas_call(
        _final_norm_kernel,
        out_shape=jax.ShapeDtypeStruct(xt.shape, xt.dtype),
        grid=(t // tm,),
        in_specs=[pl.BlockSpec((tm, d), lambda i: (i, 0)), pl.BlockSpec((1, d), lambda i: (0, 0))],
        out_specs=pl.BlockSpec((tm, d), lambda i: (i, 0)),
        compiler_params=_cparams("parallel"),
        name="final_norm",
    )(xt, g.reshape(1, d))


def _shift_rows(x, offset):
    n = x.shape[0]
    if offset == 0:
        return x
    row = lax.broadcasted_iota(jnp.int32, x.shape, 0)
    rolled = pltpu.roll(x, (-offset) % n, axis=0)
    valid = (row + offset >= 0) & (row + offset < n)
    return jnp.where(valid, rolled, 0.0)


def _dwconv_rows(x, w_ref, b_ref):
    taps = w_ref.shape[0]
    y = b_ref[...] + jnp.zeros_like(x)
    for kk in range(taps):
        y = y + w_ref[kk:kk + 1, :] * _shift_rows(x, kk - taps // 2)
    return y


def _gelu_tanh(x):
    return 0.5 * x * (1.0 + jnp.tanh(math.sqrt(2.0 / math.pi) * (x + 0.044715 * (x * x * x))))


def _rglru_kernel(x_ref, gate_ref, cw_ref, cb_ref, wa_ref, ba_ref, wx_ref, bx_ref, sp_ref, o_ref,
                  af_scr, bf_scr, ab_scr, bb_scr):
    nb, seq, _ = x_ref.shape
    row = lax.broadcasted_iota(jnp.int32, (seq, x_ref.shape[2]), 0)
    for b in range(nb):
        xc = _dwconv_rows(x_ref[b], cw_ref, cb_ref)
        xcb = xc.astype(BF16)
        for dr, (a_scr, b_scr) in enumerate(((af_scr, bf_scr), (ab_scr, bb_scr))):
            r = jax.nn.sigmoid(jnp.dot(xcb, wa_ref[dr, 0], preferred_element_type=F32) + ba_ref[dr:dr + 1, :])
            i = jax.nn.sigmoid(jnp.dot(xcb, wx_ref[dr, 0], preferred_element_type=F32) + bx_ref[dr:dr + 1, :])
            log_a = -LRU_C * r * sp_ref[dr:dr + 1, :]
            mult = jnp.sqrt(1.0 - jnp.exp(2.0 * log_a))
            mult = jnp.where(row == (seq - 1 if dr else 0), 1.0, mult)
            a_scr[b] = jnp.exp(log_a)
            b_scr[b] = mult * i * xc

    def step(tt, carry):
        new = []
        tb = seq - 1 - tt
        for b in range(nb):
            hf, hb = carry[b]
            hf = af_scr[b, pl.ds(tt, 1), :] * hf + bf_scr[b, pl.ds(tt, 1), :]
            hb = ab_scr[b, pl.ds(tb, 1), :] * hb + bb_scr[b, pl.ds(tb, 1), :]
            bf_scr[b, pl.ds(tt, 1), :] = hf
            bb_scr[b, pl.ds(tb, 1), :] = hb
            new.append((hf, hb))
        return tuple(new)

    zero = jnp.zeros((1, x_ref.shape[2]), F32)
    lax.fori_loop(0, seq, step, tuple((zero, zero) for _ in range(nb)), unroll=8)
    for b in range(nb):
        o_ref[b] = (bf_scr[b] + bb_scr[b]) * _gelu_tanh(gate_ref[b])


def _block_diag_tiles(w, tile):
    nd, nh, hd, _ = w.shape
    per = tile // hd
    w = w.reshape(nd, nh // per, per, hd, hd)
    eye = jnp.eye(per, dtype=w.dtype)
    bd = jnp.einsum('dgpij,pq->dgpiqj', w, eye).reshape(nd, nh // per, tile, tile)
    return bd.astype(BF16)


def _rglru_mixer(proj3, col_x, col_gate, conv_w, conv_b, w_a, b_a, w_x, b_x, lam):
    bsz, seq, _ = proj3.shape
    tc = V7X_LANES
    nct = LRU_W // tc
    sp = jax.nn.softplus(-lam.astype(F32))
    wa = _block_diag_tiles(w_a, tc)
    wx = _block_diag_tiles(w_x, tc)
    vec = lambda i: (0, i)
    return pl.pallas_call(
        _rglru_kernel,
        out_shape=jax.ShapeDtypeStruct((bsz, seq, LRU_W), F32),
        grid=(nct,),
        in_specs=[pl.BlockSpec((bsz, seq, tc), lambda i: (0, 0, col_x // tc + i)),
                  pl.BlockSpec((bsz, seq, tc), lambda i: (0, 0, col_gate // tc + i)),
                  pl.BlockSpec((LRU_CONV, tc), vec),
                  pl.BlockSpec((1, tc), vec),
                  pl.BlockSpec((2, 1, tc, tc), lambda i: (0, i, 0, 0)),
                  pl.BlockSpec((2, tc), vec),
                  pl.BlockSpec((2, 1, tc, tc), lambda i: (0, i, 0, 0)),
                  pl.BlockSpec((2, tc), vec),
                  pl.BlockSpec((2, tc), vec)],
        out_specs=pl.BlockSpec((bsz, seq, tc), lambda i: (0, 0, i)),
        scratch_shapes=[pltpu.VMEM((bsz, seq, tc), F32)] * 4,
        compiler_params=_cparams("parallel"),
        name="rglru",
    )(proj3, proj3, conv_w, conv_b.reshape(1, -1), wa, b_a, wx, b_x, sp)


def _dft_tables(seq):
    n = 3 * seq // 2
    kf = np.arange(n // 2, dtype=np.int64)[:, None]
    s = np.arange(seq, dtype=np.int64)[None, :]
    ang = (np.pi / n) * (((2 * kf + 1) * s) % (2 * n)).astype(np.float64)
    fc = np.cos(ang)
    fs = -np.sin(ang)
    shift = (np.pi / n) * (((2 * kf + 1) * (seq // 2)) % (2 * n)).astype(np.float64)
    to_bf16 = lambda a: jnp.asarray(a.astype(np.float32)).astype(BF16)
    return dict(n=n, fc=to_bf16(fc), fs=to_bf16(fs), fct=to_bf16(fc.T), fst=to_bf16(fs.T),
                pc=jnp.asarray(np.cos(shift).astype(np.float32)), ps=jnp.asarray(np.sin(shift).astype(np.float32)))


def _hyena_pre_kernel(x0_ref, x1_ref, v_ref, w0_ref, w1_ref, w2_ref, b0_ref, b1_ref, b2_ref,
                      vp_ref, vpb_ref, x0c_ref):
    x0c_ref[0] = _dwconv_rows(x0_ref[0], w0_ref, b0_ref)
    vp = _dwconv_rows(v_ref[0], w2_ref, b2_ref) * _dwconv_rows(x1_ref[0], w1_ref, b1_ref)
    vp_ref[0] = vp
    vpb_ref[0] = vp.astype(BF16)


def _hyena_filter_kernel(z_ref, t_ref, dl_ref, w1_ref, b1_ref, w2_ref, b2_ref, w3_ref, b3_ref, w4_ref, fr_ref,
                         fc_ref, fs_ref, pc_ref, ps_ref, hr_ref, hi_ref, filt_scr):
    @pl.when(pl.program_id(0) == 0)
    def _():
        fr = fr_ref[...]
        hid = jnp.sin(fr * (jnp.dot(z_ref[...].astype(BF16), w1_ref[...].astype(BF16),
                                    preferred_element_type=F32) + b1_ref[...]))
        hid = jnp.sin(fr * (jnp.dot(hid.astype(BF16), w2_ref[...].astype(BF16),
                                    preferred_element_type=F32) + b2_ref[...]))
        hid = jnp.sin(fr * (jnp.dot(hid.astype(BF16), w3_ref[...].astype(BF16),
                                    preferred_element_type=F32) + b3_ref[...]))
        filt = jnp.dot(hid.astype(BF16), w4_ref[...].astype(BF16), preferred_element_type=F32)
        filt = filt * jnp.exp(-2.0 * jnp.abs(t_ref[...] - 0.5) * dl_ref[...])
        filt = filt / jnp.sum(jnp.abs(filt), axis=0, keepdims=True)
        filt_scr[...] = filt.astype(BF16)

    f = filt_scr[...]
    hr0 = jnp.dot(fc_ref[...], f, preferred_element_type=F32)
    hi0 = jnp.dot(fs_ref[...], f, preferred_element_type=F32)
    pc = pc_ref[...]
    ps = ps_ref[...]
    hr_ref[...] = hr0 * pc - hi0 * ps
    hi_ref[...] = hr0 * ps + hi0 * pc


def _hyena_fwd_kernel(fc_ref, fs_ref, vp_ref, hr_ref, hi_ref, yr_ref, yi_ref):
    v = vp_ref[0]
    vr = jnp.dot(fc_ref[...], v, preferred_element_type=F32)
    vi = jnp.dot(fs_ref[...], v, preferred_element_type=F32)
    hr = hr_ref[...]
    hi = hi_ref[...]
    yr_ref[0] = (vr * hr - vi * hi).astype(BF16)
    yi_ref[0] = (vr * hi + vi * hr).astype(BF16)


def _hyena_inv_kernel(fct_ref, fst_ref, yr_ref, yi_ref, vp_ref, x0c_ref, bias_ref, o_ref, *, inv_scale):
    y = (jnp.dot(fct_ref[...], yr_ref[0], preferred_element_type=F32)
         + jnp.dot(fst_ref[...], yi_ref[0], preferred_element_type=F32)) * inv_scale
    o_ref[0] = (y + vp_ref[0] * bias_ref[...]) * x0c_ref[0]


def _hyena_mixer_pallas(proj3, col0, conv_w, conv_b, fw1, fb1, fw2, fb2, fw3, fb3, fw4, freq, fft_bias,
                        tk=512, tt=512):
    bsz, seq, _ = proj3.shape
    w = fw4.shape[1]
    tc = V7X_LANES
    nct = w // tc
    cb0 = col0 // tc
    tab = _dft_tables(seq)
    nfreq = tab['n'] // 2
    cwb = lambda off: pl.BlockSpec((HY_SHORT, tc), lambda b, c: (0, off + c))
    cbb = lambda off: pl.BlockSpec((1, tc), lambda b, c: (0, off + c))
    xb = lambda off: pl.BlockSpec((1, seq, tc), lambda b, c: (b, 0, cb0 + off + c))
    ob = pl.BlockSpec((1, seq, tc), lambda b, c: (b, 0, c))
    cb2 = conv_b.reshape(1, -1)
    vp, vpb, x0c = pl.pallas_call(
        _hyena_pre_kernel,
        out_shape=(jax.ShapeDtypeStruct((bsz, seq, w), F32), jax.ShapeDtypeStruct((bsz, seq, w), BF16),
                   jax.ShapeDtypeStruct((bsz, seq, w), F32)),
        grid=(bsz, nct),
        in_specs=[xb(0), xb(nct), xb(2 * nct), cwb(0), cwb(nct), cwb(2 * nct), cbb(0), cbb(nct), cbb(2 * nct)],
        out_specs=(ob, ob, ob),
        compiler_params=_cparams("parallel", "parallel"),
        name="hyena_pre",
    )(proj3, proj3, proj3, conv_w, conv_w, conv_w, cb2, cb2, cb2)

    tcol = np.linspace(0.0, 1.0, seq, dtype=np.float32)[:, None]
    wcol = ((2.0 * math.pi / seq) * np.arange(seq, dtype=np.float32))[:, None]
    bands = np.linspace(1e-4, HY_BANDS - 1, HY_BANDS, dtype=np.float32)[None, :]
    z = np.concatenate([tcol, np.cos(bands * wcol), -np.sin(bands * wcol)], axis=-1).astype(np.float32)
    zp = np.zeros((seq, tc), np.float32)
    zp[:, :HY_EMB] = z
    w1p = jnp.zeros((tc, HY_ORDER), F32).at[:HY_EMB].set(fw1)
    deltas = np.abs(np.linspace(math.log(HY_TARGET) / HY_SLOW, math.log(HY_TARGET) / HY_FAST, w,
                                dtype=np.float32))[None, :]
    full = lambda a: pl.BlockSpec(a.shape, lambda j: (0,) * a.ndim)
    row = lambda a: a.reshape(1, -1)
    small = [jnp.asarray(zp), jnp.asarray(tcol), jnp.asarray(deltas), w1p, row(fb1), fw2, row(fb2), fw3, row(fb3),
             fw4, row(freq)]
    hr, hi = pl.pallas_call(
        _hyena_filter_kernel,
        out_shape=(jax.ShapeDtypeStruct((nfreq, w), F32), jax.ShapeDtypeStruct((nfreq, w), F32)),
        grid=(nfreq // tk,),
        in_specs=[full(a) for a in small]
                 + [pl.BlockSpec((tk, seq), lambda j: (j, 0)), pl.BlockSpec((tk, seq), lambda j: (j, 0)),
                    pl.BlockSpec((tk, 1), lambda j: (j, 0)), pl.BlockSpec((tk, 1), lambda j: (j, 0))],
        out_specs=(pl.BlockSpec((tk, w), lambda j: (j, 0)), pl.BlockSpec((tk, w), lambda j: (j, 0))),
        scratch_shapes=[pltpu.VMEM((seq, w), BF16)],
        compiler_params=_cparams("arbitrary"),
        name="hyena_filter",
    )(*small, tab['fc'], tab['fs'], tab['pc'], tab['ps'])

    yr, yi = pl.pallas_call(
        _hyena_fwd_kernel,
        out_shape=(jax.ShapeDtypeStruct((bsz, nfreq, w), BF16), jax.ShapeDtypeStruct((bsz, nfreq, w), BF16)),
        grid=(bsz, nfreq // tk),
        in_specs=[pl.BlockSpec((tk, seq), lambda b, j: (j, 0)), pl.BlockSpec((tk, seq), lambda b, j: (j, 0)),
                  pl.BlockSpec((1, seq, w), lambda b, j: (b, 0, 0)),
                  pl.BlockSpec((tk, w), lambda b, j: (j, 0)), pl.BlockSpec((tk, w), lambda b, j: (j, 0))],
        out_specs=(pl.BlockSpec((1, tk, w), lambda b, j: (b, j, 0)), pl.BlockSpec((1, tk, w), lambda b, j: (b, j, 0))),
        compiler_params=_cparams("parallel", "parallel"),
        name="hyena_fwd",
    )(tab['fc'], tab['fs'], vpb, hr, hi)

    return pl.pallas_call(
        partial(_hyena_inv_kernel, inv_scale=2.0 / tab['n']),
        out_shape=jax.ShapeDtypeStruct((bsz, seq, w), F32),
        grid=(bsz, seq // tt),
        in_specs=[pl.BlockSpec((tt, nfreq), lambda b, j: (j, 0)), pl.BlockSpec((tt, nfreq), lambda b, j: (j, 0)),
                  pl.BlockSpec((1, nfreq, w), lambda b, j: (b, 0, 0)), pl.BlockSpec((1, nfreq, w), lambda b, j: (b, 0, 0)),
                  pl.BlockSpec((1, tt, w), lambda b, j: (b, j, 0)), pl.BlockSpec((1, tt, w), lambda b, j: (b, j, 0)),
                  pl.BlockSpec((1, w), lambda b, j: (0, 0))],
        out_specs=pl.BlockSpec((1, tt, w), lambda b, j: (b, j, 0)),
        compiler_params=_cparams("parallel", "parallel"),
        name="hyena_inv",
    )(tab['fct'], tab['fst'], yr, yi, vp, x0c, fft_bias.reshape(1, w))


def _dwconv_centred(x, w, b):
    k = w.shape[0]
    left = k // 2
    right = k - 1 - left
    y = lax.conv_general_dilated(x, w[:, None, :].astype(x.dtype), window_strides=(1,),
                                 padding=[(left, right)], dimension_numbers=('NWC', 'WIO', 'NWC'),
                                 feature_group_count=x.shape[-1])
    return y + b.astype(x.dtype)


def _t5_bucket(rel):
    nb = REL_BUCKETS // 2
    ret = (rel > 0).astype(jnp.int32) * nb
    n = jnp.abs(rel)
    max_exact = nb // 2
    large = max_exact + (jnp.log(jnp.maximum(n, 1).astype(jnp.float32) / max_exact)
                         / math.log(REL_MAX_DIST / max_exact) * (nb - max_exact)).astype(jnp.int32)
    large = jnp.minimum(large, nb - 1)
    return ret + jnp.where(n < max_exact, n, large)


def _ssd_chunked(x, dt, a, bm, cm):
    bsz, seq_len, n_heads, hd = x.shape
    g = bm.shape[2]
    j = n_heads // g
    n = bm.shape[-1]
    q = SSD_CHUNK
    c = seq_len // q
    xd = (x.astype(jnp.float32) * dt[..., None]).reshape(bsz, c, q, g, j, hd)
    a_cum = jnp.cumsum((dt * a).reshape(bsz, c, q, g, j), axis=2)
    bc = bm.astype(jnp.float32).reshape(bsz, c, q, g, n)
    cc = cm.astype(jnp.float32).reshape(bsz, c, q, g, n)
    seg = a_cum[:, :, :, None] - a_cum[:, :, None, :]
    lower = jnp.tril(jnp.ones((q, q), dtype=bool))[:, :, None, None]
    l_mat = jnp.exp(jnp.where(lower, seg, -jnp.inf))
    cb = jnp.einsum('bclgn,bcsgn->bclsg', cc, bc)
    y_diag = jnp.einsum('bclsgj,bcsgjp->bclgjp', cb[..., None] * l_mat, xd)
    decay_s = jnp.exp(a_cum[:, :, -1:] - a_cum)
    states = jnp.einsum('bclgn,bclgjp->bcgjpn', bc, xd * decay_s[..., None])
    chunk_decay = jnp.exp(a_cum[:, :, -1])

    def step(s, inp):
        dec, st = inp
        return dec[..., None, None] * s + st, s

    s0 = jnp.zeros((bsz, g, j, hd, n), jnp.float32)
    _, prev = lax.scan(step, s0, (jnp.moveaxis(chunk_decay, 1, 0), jnp.moveaxis(states, 1, 0)))
    prev = jnp.moveaxis(prev, 0, 1)
    y_off = jnp.einsum('bclgn,bcgjpn->bclgjp', cc, prev) * jnp.exp(a_cum)[..., None]
    return (y_diag + y_off).reshape(bsz, seq_len, n_heads, hd)


def _ssd_mixer(z, xbc, dt_raw, conv_w, conv_b, dt_bias, a_log, d_skip, norm_g):
    bsz, seq_len, _ = z.shape
    xbc = jax.nn.silu(_dwconv_centred(xbc, conv_w, conv_b))
    xs, bm, cm = jnp.split(xbc, [SSD_W, SSD_W + SSD_GROUPS * SSD_STATE], axis=-1)
    xh = xs.reshape(bsz, seq_len, SSD_HEADS, SSD_HD)
    bm = bm.reshape(bsz, seq_len, SSD_GROUPS, SSD_STATE)
    cm = cm.reshape(bsz, seq_len, SSD_GROUPS, SSD_STATE)
    dts = jax.nn.softplus(dt_raw.astype(jnp.float32).reshape(bsz, seq_len, 2, SSD_HEADS)
                          + dt_bias.astype(jnp.float32))
    a = -jnp.exp(a_log.astype(jnp.float32))
    y_f = _ssd_chunked(xh, dts[:, :, 0], a[0], bm, cm)
    fl = lambda t: jnp.flip(t, axis=1)
    y_b = fl(_ssd_chunked(fl(xh), fl(dts[:, :, 1]), a[1], fl(bm), fl(cm)))
    y = y_f + y_b + d_skip.astype(jnp.float32)[:, None] * xh.astype(jnp.float32)
    y = y.reshape(bsz, seq_len, SSD_W) * jax.nn.silu(z.astype(jnp.float32))
    yg = y.reshape(bsz, seq_len, SSD_GROUPS, SSD_W // SSD_GROUPS)
    yg = yg * lax.rsqrt(jnp.mean(yg * yg, axis=-1, keepdims=True) + RMS_EPS)
    return (yg.reshape(bsz, seq_len, SSD_W) * norm_g.astype(jnp.float32)).astype(z.dtype)


MIX_COLS = HY_COLS + LRU_COLS + DA_COLS + SSD_W + SSD_XBC
DT_COLS = 2 * SSD_HEADS
MIX_PAD = MIX_COLS + V7X_LANES


def kernel(x, norm1_g, w_in, hy_conv_w, hy_conv_b, hy_fw1, hy_fb1, hy_fw2, hy_fb2, hy_fw3, hy_fb3,
           hy_fw4, hy_freq, hy_bias, lru_conv_w, lru_conv_b, lru_wa, lru_ba, lru_wx, lru_bx, lru_lam,
           da_lam, da_subln_g, ssd_conv_w, ssd_conv_b, ssd_dt_bias, ssd_a_log, ssd_d, ssd_norm_g,
           w_branch, b_gate, w_out, norm2_g, w_router, moe_w1, moe_w3, moe_w2, rel_bias, final_g):
    bsz, seq_len, d_model = x.shape
    t = bsz * seq_len
    xt = x.reshape(t, d_model)
    o1 = HY_COLS
    o2 = o1 + LRU_COLS
    o3 = o2 + DA_COLS
    o4 = o3 + SSD_W + SSD_XBC
    for l in range(DEPTH):
        g1 = norm1_g[l].reshape(1, d_model)
        w_mix = w_in[l][:, :MIX_COLS].astype(BF16)
        w_dt = jnp.pad(w_in[l][:, MIX_COLS:MIX_COLS + DT_COLS].astype(BF16), ((0, 0), (0, V7X_LANES - DT_COLS)))
        proj, dt_raw = _inproj(xt, g1, w_mix, w_dt)
        p3 = proj.reshape(bsz, seq_len, MIX_COLS)
        o_hy = _hyena_mixer_pallas(p3, 0, hy_conv_w[l], hy_conv_b[l], hy_fw1[l], hy_fb1[l], hy_fw2[l],
                                   hy_fb2[l], hy_fw3[l], hy_fb3[l], hy_fw4[l], hy_freq[l], hy_bias[l])
        o_lru = _rglru_mixer(p3, o1, o1 + LRU_W, lru_conv_w[l], lru_conv_b[l],
                             lru_wa[l], lru_ba[l], lru_wx[l], lru_bx[l], lru_lam[l])
        lam_init = 0.8 - 0.6 * math.exp(-0.3 * l)
        o_da = _diff_attention(proj, o2, bsz, seq_len, da_lam[l], da_subln_g[l], rel_bias, lam_init)
        o_ssd = _ssd_mixer(p3[..., o3:o3 + SSD_W], p3[..., o3 + SSD_W:o4],
                           dt_raw[:, :DT_COLS].reshape(bsz, seq_len, DT_COLS),
                           ssd_conv_w[l], ssd_conv_b[l], ssd_dt_bias[l], ssd_a_log[l], ssd_d[l],
                           ssd_norm_g[l])
        wg = w_in[l][:, MIX_COLS + DT_COLS:].astype(BF16)
        branches = [o_hy.reshape(t, BR_W), o_lru.reshape(t, BR_W), o_da, o_ssd.reshape(t, BR_W)]
        xt = _gated_merge(xt, g1, wg, b_gate[l], branches, w_branch[l].astype(BF16),
                          w_out[l].astype(BF16))
        xt = _expert_choice_ffn(xt, bsz, seq_len, norm2_g[l], w_router[l], moe_w1[l], moe_w3[l], moe_w2[l])
    return _final_norm(xt, final_g).reshape(bsz, seq_len, d_model)
```

```python
import math
from functools import partial

import numpy as np
import jax
import jax.numpy as jnp
from jax import lax
from jax.experimental import pallas as pl
from jax.experimental.pallas import tpu as pltpu

D_MODEL = 2048
BATCH = 4
SEQ = 2048
DEPTH = 2

N_BRANCH = 4
BR_W = D_MODEL // 4
RMS_EPS = 1e-6

HY_W = BR_W
HY_SHORT = 3
HY_EMB = 33
HY_BANDS = (HY_EMB - 1) // 2
HY_ORDER = 64
HY_TARGET = 1e-2
HY_FAST = 0.3
HY_SLOW = 1.5

LRU_W = BR_W
LRU_HEADS = 8
LRU_HD = LRU_W // LRU_HEADS
LRU_CONV = 4
LRU_C = 8.0

DA_HEADS = 4
DA_HD = BR_W // (2 * DA_HEADS)
DA_QBLOCK = 128
REL_BUCKETS = 32
REL_MAX_DIST = 128

SSD_W = BR_W
SSD_HD = 64
SSD_HEADS = SSD_W // SSD_HD
SSD_GROUPS = 2
SSD_STATE = 128
SSD_CONV = 4
SSD_CHUNK = 128

N_EXPERTS = 16
EC_CAPACITY = 2
D_EXPERT = D_MODEL

HY_COLS = 3 * HY_W
LRU_COLS = 2 * LRU_W
DA_COLS = 3 * DA_HEADS * 2 * DA_HD
SSD_XBC = SSD_W + 2 * SSD_GROUPS * SSD_STATE
SSD_COLS = SSD_W + SSD_XBC + 2 * SSD_HEADS
GATE_COLS = N_BRANCH * D_MODEL
IN_COLS = HY_COLS + LRU_COLS + DA_COLS + SSD_COLS + GATE_COLS

V7X_LANES = 128
V7X_VMEM_BYTES = 64 * 1024 * 1024
VMEM_LIMIT_BYTES = V7X_VMEM_BYTES - 8 * 1024 * 1024

BF16 = jnp.bfloat16
F32 = jnp.float32


def _cparams(*sem):
    return pltpu.CompilerParams(dimension_semantics=sem, vmem_limit_bytes=VMEM_LIMIT_BYTES)


def _rms_rows(x, g, eps):
    ms = jnp.mean(x * x, axis=-1, keepdims=True)
    return x * lax.rsqrt(ms + eps) * g


def _cast_kernel(w_ref, o_ref):
    o_ref[...] = w_ref[0].astype(BF16)


def _cast_layer_weight(w, layer, tn=512):
    _, r, c = w.shape
    return pl.pallas_call(
        _cast_kernel,
        out_shape=jax.ShapeDtypeStruct((r, c), BF16),
        grid=(pl.cdiv(c, tn),),
        in_specs=[pl.BlockSpec((1, r, tn), lambda j: (layer, 0, j))],
        out_specs=pl.BlockSpec((r, tn), lambda j: (0, j)),
        compiler_params=_cparams("parallel"),
        name="cast_weight",
    )(w)


def _inproj_kernel(x_ref, g_ref, w_ref, ws_ref, o_ref, os_ref, h_scr):
    @pl.when(pl.program_id(1) == 0)
    def _():
        h_scr[...] = _rms_rows(x_ref[...], g_ref[...], RMS_EPS).astype(BF16)
        os_ref[...] = jnp.dot(h_scr[...], ws_ref[...], preferred_element_type=F32)

    o_ref[...] = jnp.dot(h_scr[...], w_ref[...], preferred_element_type=F32)


def _inproj(xt, g, wq, n, tm=1024, tn=512):
    t, d = xt.shape
    ns = V7X_LANES
    return pl.pallas_call(
        _inproj_kernel,
        out_shape=(jax.ShapeDtypeStruct((t, n), F32), jax.ShapeDtypeStruct((t, ns), F32)),
        grid=(t // tm, n // tn),
        in_specs=[pl.BlockSpec((tm, d), lambda i, j: (i, 0)),
                  pl.BlockSpec((1, d), lambda i, j: (0, 0)),
                  pl.BlockSpec((d, tn), lambda i, j: (0, j)),
                  pl.BlockSpec((d, ns), lambda i, j: (0, n // ns))],
        out_specs=(pl.BlockSpec((tm, tn), lambda i, j: (i, j)), pl.BlockSpec((tm, ns), lambda i, j: (i, 0))),
        scratch_shapes=[pltpu.VMEM((tm, d), BF16)],
        compiler_params=_cparams("parallel", "arbitrary"),
        name="inproj",
    )(xt, g, wq, wq)


def _diffattn_kernel(relb_ref, q_ref, k_ref, v_ref, bucket_ref, lamqk_ref, g_ref, o_ref, *,
                     tq, lam_init, head_dim):
    h = pl.program_id(1)
    seq = q_ref.shape[0]
    margin = REL_MAX_DIST
    bucket = bucket_ref[...]
    band = jnp.zeros(bucket.shape, F32)
    for b in range(REL_BUCKETS):
        band = jnp.where(bucket == b, relb_ref[h, b], band)
    c_neg = band[0:1, 0:1]
    c_pos = band[tq - 1:tq, tq + 2 * margin - 1:tq + 2 * margin]

    lq = lamqk_ref[...]
    lam = (jnp.exp(jnp.sum(lq[0:1] * lq[1:2], axis=-1, keepdims=True))
           - jnp.exp(jnp.sum(lq[2:3] * lq[3:4], axis=-1, keepdims=True)) + lam_init)

    lane = lax.broadcasted_iota(jnp.int32, (1, 2 * head_dim), 1)
    lo_mask = (lane < head_dim).astype(F32)
    hi_mask = 1.0 - lo_mask
    k = k_ref[...].astype(BF16)
    v = v_ref[...].astype(BF16)
    scale = head_dim ** -0.5
    dn = (((1,), (1,)), ((), ()))
    for i in range(seq // tq):
        q0 = i * tq
        q = q_ref[q0:q0 + tq, :] * scale
        q1 = (q * lo_mask).astype(BF16)
        q2 = (q * hi_mask).astype(BF16)
        lo = max(q0 - margin, 0)
        hi = min(q0 + tq + margin, seq)
        pieces = []
        if lo > 0:
            pieces.append(jnp.broadcast_to(c_neg, (tq, lo)))
        pieces.append(band[:, lo - (q0 - margin):hi - (q0 - margin)])
        if hi < seq:
            pieces.append(jnp.broadcast_to(c_pos, (tq, seq - hi)))
        bias = jnp.concatenate(pieces, axis=1) if len(pieces) > 1 else pieces[0]
        s1 = lax.dot_general(q1, k, dn, preferred_element_type=F32) + bias
        s2 = lax.dot_general(q2, k, dn, preferred_element_type=F32) + bias
        p1 = jnp.exp(s1 - jnp.max(s1, axis=-1, keepdims=True))
        p2 = jnp.exp(s2 - jnp.max(s2, axis=-1, keepdims=True))
        r1 = 1.0 / jnp.sum(p1, axis=-1, keepdims=True)
        r2 = lam / jnp.sum(p2, axis=-1, keepdims=True)
        a = p1 * r1 - p2 * r2
        o = jnp.dot(a.astype(BF16), v, preferred_element_type=F32)
        o = _rms_rows(o, g_ref[...], 1e-5) * (1.0 - lam_init)
        o_ref[q0:q0 + tq, :] = o


def _diff_attention(proj, col0, bsz, seq, lam_qk, subln_g, rel_bias, lam_init, tq=256):
    hw = 2 * DA_HD
    cb = col0 // hw
    margin = REL_MAX_DIST
    r = jnp.arange(tq)[:, None]
    c = jnp.arange(tq + 2 * margin)[None, :]
    bucket = _t5_bucket(c - margin - r)
    kern = partial(_diffattn_kernel, tq=tq, lam_init=lam_init, head_dim=DA_HD)
    return pl.pallas_call(
        kern,
        out_shape=jax.ShapeDtypeStruct((bsz * seq, DA_HEADS * hw), F32),
        grid=(bsz, DA_HEADS),
        in_specs=[pl.BlockSpec(memory_space=pltpu.SMEM),
                  pl.BlockSpec((seq, hw), lambda b, h: (b, cb + h)),
                  pl.BlockSpec((seq, hw), lambda b, h: (b, cb + DA_HEADS + h)),
                  pl.BlockSpec((seq, hw), lambda b, h: (b, cb + 2 * DA_HEADS + h)),
                  pl.BlockSpec(bucket.shape, lambda b, h: (0, 0)),
                  pl.BlockSpec(lam_qk.shape, lambda b, h: (0, 0)),
                  pl.BlockSpec((1, hw), lambda b, h: (0, 0))],
        out_specs=pl.BlockSpec((seq, hw), lambda b, h: (b, h)),
        compiler_params=_cparams("parallel", "parallel"),
        name="diffattn",
    )(rel_bias.T, proj, proj, proj, bucket, lam_qk, subln_g.reshape(1, hw))


def _merge_kernel(x_ref, g_ref, wg0_ref, wg1_ref, wg2_ref, wg3_ref, bg_ref, b0_ref, b1_ref, b2_ref, b3_ref,
                  wb_ref, wo_ref, o_ref, h_scr, br_scr, acc_scr, *, shift):
    j = pl.program_id(1)
    wg_ref = (wg0_ref, wg1_ref, wg2_ref, wg3_ref)

    @pl.when(j == 0)
    def _():
        h_scr[...] = _rms_rows(x_ref[...], g_ref[...], RMS_EPS).astype(BF16)
        for kk, b_ref in enumerate((b0_ref, b1_ref, b2_ref, b3_ref)):
            br_scr[kk] = b_ref[...].astype(BF16)
        acc_scr[...] = jnp.zeros_like(acc_scr)

    h = h_scr[...]
    tc = wo_ref.shape[0]
    c_pos = j * tc + lax.broadcasted_iota(jnp.int32, (1, tc), 1)
    valid = (c_pos >= shift) & (c_pos < shift + x_ref.shape[1])
    m = None
    for kk in range(N_BRANCH):
        gate = jnp.dot(h, wg_ref[kk][...], preferred_element_type=F32) + bg_ref[kk:kk + 1, :]
        bp = jnp.dot(br_scr[kk], wb_ref[kk], preferred_element_type=F32)
        term = jax.nn.sigmoid(gate) * bp
        m = term if m is None else m + term
    m = jnp.where(valid, m, 0.0)
    acc_scr[...] += jnp.dot(m.astype(BF16), wo_ref[...], preferred_element_type=F32)

    @pl.when(j == pl.num_programs(1) - 1)
    def _():
        o_ref[...] = x_ref[...] + acc_scr[...]


def _gated_merge(xt, g, wq, gate_col0, bg, branches, wb, wo, tm=512, tc=256):
    t, d = xt.shape
    bw = branches[0].shape[1]
    shift = gate_col0 % tc
    blk0 = gate_col0 // tc
    ncb = d // tc + 1
    dp = ncb * tc
    bg_s = jnp.pad(bg, ((0, 0), (shift, dp - d - shift)))
    wb_s = jnp.pad(wb, ((0, 0), (0, 0), (shift, dp - d - shift)))
    wo_s = jnp.pad(wo, ((shift, dp - d - shift), (0, 0)))
    return pl.pallas_call(
        partial(_merge_kernel, shift=shift),
        out_shape=jax.ShapeDtypeStruct((t, d), F32),
        grid=(t // tm, ncb),
        in_specs=[pl.BlockSpec((tm, d), lambda i, j: (i, 0)),
                  pl.BlockSpec((1, d), lambda i, j: (0, 0))]
                 + [pl.BlockSpec((d, tc), lambda i, j, kk=kk: (0, blk0 + kk * (d // tc) + j))
                    for kk in range(N_BRANCH)]
                 + [pl.BlockSpec((N_BRANCH, tc), lambda i, j: (0, j))]
                 + [pl.BlockSpec((tm, bw), lambda i, j: (i, 0))] * N_BRANCH
                 + [pl.BlockSpec((N_BRANCH, bw, tc), lambda i, j: (0, 0, j)),
                    pl.BlockSpec((tc, d), lambda i, j: (j, 0))],
        out_specs=pl.BlockSpec((tm, d), lambda i, j: (i, 0)),
        scratch_shapes=[pltpu.VMEM((tm, d), BF16), pltpu.VMEM((N_BRANCH, tm, bw), BF16),
                        pltpu.VMEM((tm, d), F32)],
        compiler_params=_cparams("parallel", "arbitrary"),
        name="gated_merge",
    )(xt, g, wq, wq, wq, wq, bg_s, *branches, wb_s, wo_s)


def _pack_bf16_halves(h):
    half = h.shape[1] // 2
    bits = lax.bitcast_convert_type(h.astype(F32), jnp.uint32)
    return (bits[:, :half] >> 16) | (bits[:, half:] & jnp.uint32(0xFFFF0000))


def _unpack_bf16_halves(p):
    lo = lax.bitcast_convert_type(p << 16, F32).astype(BF16)
    hi = lax.bitcast_convert_type(p & jnp.uint32(0xFFFF0000), F32).astype(BF16)
    return lo, hi


def _router_kernel(x_ref, g_ref, wr_ref, h_ref, aff_ref, *, n_experts):
    h = _rms_rows(x_ref[...], g_ref[...], RMS_EPS).astype(BF16)
    h_ref[...] = _pack_bf16_halves(h)
    logits = jnp.dot(h, wr_ref[...], preferred_element_type=F32)
    lane = lax.broadcasted_iota(jnp.int32, logits.shape, 1)
    logits = jnp.where(lane < n_experts, logits, -jnp.inf)
    p = jnp.exp(logits - jnp.max(logits, axis=-1, keepdims=True))
    aff_ref[...] = p / jnp.sum(p, axis=-1, keepdims=True)


def _norm_router(xt, g, w_router, tm=512):
    t, d = xt.shape
    e = w_router.shape[1]
    wr = jnp.zeros((d, V7X_LANES), BF16).at[:, :e].set(w_router.astype(BF16))
    return pl.pallas_call(
        partial(_router_kernel, n_experts=e),
        out_shape=(jax.ShapeDtypeStruct((t, d // 2), jnp.uint32), jax.ShapeDtypeStruct((t, V7X_LANES), F32)),
        grid=(t // tm,),
        in_specs=[pl.BlockSpec((tm, d), lambda i: (i, 0)),
                  pl.BlockSpec((1, d), lambda i: (0, 0)),
                  pl.BlockSpec((d, V7X_LANES), lambda i: (0, 0))],
        out_specs=(pl.BlockSpec((tm, d // 2), lambda i: (i, 0)),
                   pl.BlockSpec((tm, V7X_LANES), lambda i: (i, 0))),
        compiler_params=_cparams("parallel"),
        name="norm_router",
    )(xt, g, wr)


def _expert_kernel(xg_ref, w1_ref, w3_ref, w2_ref, gate_ref, o_ref, x_scr, acc_scr):
    f = pl.program_id(1)
    half = x_scr.shape[1] // 2

    @pl.when(f == 0)
    def _():
        lanes = xg_ref.shape[3]
        for s in range(xg_ref.shape[2]):
            lo, hi = _unpack_bf16_halves(xg_ref[0, :, s, :])
            x_scr[:, s * lanes:(s + 1) * lanes] = lo
            x_scr[:, half + s * lanes:half + (s + 1) * lanes] = hi
        acc_scr[...] = jnp.zeros_like(acc_scr)

    xg = x_scr[...]
    a = jnp.dot(xg, w1_ref[0].astype(BF16), preferred_element_type=F32)
    b = jnp.dot(xg, w3_ref[0].astype(BF16), preferred_element_type=F32)
    hid = (a * jax.nn.sigmoid(a) * b).astype(BF16)
    acc_scr[...] += jnp.dot(hid, w2_ref[0].astype(BF16), preferred_element_type=F32)

    @pl.when(f == pl.num_programs(1) - 1)
    def _():
        o_ref[0] = (acc_scr[...] * gate_ref[0]).astype(BF16)


def _experts(xg, w1, w3, w2, gate, tf=256):
    e, c, sub, lanes = xg.shape
    half = sub * lanes
    d = 2 * half
    f = w1.shape[2]
    return pl.pallas_call(
        _expert_kernel,
        out_shape=jax.ShapeDtypeStruct((e, c, d), BF16),
        grid=(e, f // tf),
        in_specs=[pl.BlockSpec((1, c, sub, lanes), lambda i, j: (i, 0, 0, 0)),
                  pl.BlockSpec((1, d, tf), lambda i, j: (i, 0, j)),
                  pl.BlockSpec((1, d, tf), lambda i, j: (i, 0, j)),
                  pl.BlockSpec((1, tf, d), lambda i, j: (i, j, 0)),
                  pl.BlockSpec((1, c, 1), lambda i, j: (i, 0, 0))],
        out_specs=pl.BlockSpec((1, c, d), lambda i, j: (i, 0, 0)),
        scratch_shapes=[pltpu.VMEM((c, d), BF16), pltpu.VMEM((c, d), F32)],
        compiler_params=_cparams("parallel", "arbitrary"),
        name="experts",
    )(xg, w1, w3, w2, gate)


def _prefix_count(mask_f, tri):
    rows, n = mask_f.shape
    w = tri.shape[0]
    run = jnp.zeros((rows, 1), F32)
    outs = []
    for c in range(n // w):
        blk = mask_f[:, c * w:(c + 1) * w]
        outs.append(jnp.dot(blk.astype(BF16), tri, preferred_element_type=F32) + run)
        run = run + jnp.sum(blk, axis=-1, keepdims=True)
    return jnp.concatenate(outs, axis=1), run


def _topk_kernel(aff_ref, idx_ref, gate_ref, *, cap):
    aff = aff_ref[0]
    n_exp, n_tok = aff.shape
    keys = lax.bitcast_convert_type(aff, jnp.int32)
    thr = jnp.zeros((n_exp, 1), jnp.int32)
    for bit in range(30, -1, -1):
        cand = thr | (1 << bit)
        cnt = jnp.sum((keys >= cand).astype(F32), axis=-1, keepdims=True)
        thr = jnp.where(cnt >= cap, cand, thr)
    gt = (keys > thr).astype(F32)
    eq = (keys == thr).astype(F32)
    w = V7X_LANES
    r_i = lax.broadcasted_iota(jnp.int32, (w, w), 0)
    c_i = lax.broadcasted_iota(jnp.int32, (w, w), 1)
    tri = (r_i < c_i).astype(BF16)
    need = cap - jnp.sum(gt, axis=-1, keepdims=True)
    eq_rank, _ = _prefix_count(eq, tri)
    sel = gt + eq * (eq_rank < need).astype(F32)
    pos, _ = _prefix_count(sel, tri)
    tok = lax.broadcasted_iota(jnp.int32, (1, n_tok), 1)
    a_h = aff.astype(BF16)
    rem = aff - a_h.astype(F32)
    a_m = rem.astype(BF16)
    a_l = (rem - a_m.astype(F32)).astype(BF16)
    slot = lax.broadcasted_iota(jnp.int32, (cap, n_tok), 0).astype(F32)
    pos = jnp.where(sel > 0.5, pos, -1.0)
    dn = (((1,), (1,)), ((), ()))
    for e in range(n_exp):
        hit = jnp.where(pos[e:e + 1, :] == slot, 1.0, 0.0).astype(BF16)
        src = jnp.concatenate([(tok >> 6).astype(F32), (tok & 63).astype(F32),
                               a_h[e:e + 1, :].astype(F32), a_m[e:e + 1, :].astype(F32),
                               a_l[e:e + 1, :].astype(F32), jnp.zeros((3, n_tok), F32)], axis=0).astype(BF16)
        res = lax.dot_general(src, hit, dn, preferred_element_type=F32)
        idx_ref[0, e:e + 1, :] = (res[0:1] * 64.0 + res[1:2]).astype(jnp.int32)
        gate_ref[0, e:e + 1, :] = res[2:3] + res[3:4] + res[4:5]


def _topk_select(aff_t, cap):
    bsz, n_exp, n_tok = aff_t.shape
    blk = lambda n: pl.BlockSpec((1, n_exp, n), lambda b: (b, 0, 0))
    return pl.pallas_call(
        partial(_topk_kernel, cap=cap),
        out_shape=(jax.ShapeDtypeStruct((bsz, n_exp, cap), jnp.int32),
                   jax.ShapeDtypeStruct((bsz, n_exp, cap), F32)),
        grid=(bsz,),
        in_specs=[blk(n_tok)],
        out_specs=(blk(cap), blk(cap)),
        compiler_params=_cparams("parallel"),
        name="topk_select",
    )(aff_t)


def _gather_kernel(rows_ref, h_hbm, o_ref, sem):
    e = pl.program_id(0)
    n = o_ref.shape[1]

    def row_copy(j):
        return pltpu.make_async_copy(h_hbm.at[rows_ref[e, j]], o_ref.at[0, j], sem)

    def start(j, carry):
        row_copy(j).start()
        return carry

    lax.fori_loop(0, n, start, 0, unroll=8)
    pltpu.make_async_copy(h_hbm.at[pl.ds(0, n)], o_ref.at[0], sem).wait()


def _moe_gather(h, rows):
    n_exp, c = rows.shape
    _, sub, lanes = h.shape
    return pl.pallas_call(
        _gather_kernel,
        out_shape=jax.ShapeDtypeStruct((n_exp, c, sub, lanes), h.dtype),
        grid_spec=pltpu.PrefetchScalarGridSpec(
            num_scalar_prefetch=1,
            grid=(n_exp,),
            in_specs=[pl.BlockSpec(memory_space=pl.ANY)],
            out_specs=pl.BlockSpec((1, c, sub, lanes), lambda e, rows: (e, 0, 0, 0)),
            scratch_shapes=[pltpu.SemaphoreType.DMA(())]),
        compiler_params=_cparams("arbitrary"),
        name="moe_gather",
    )(rows, h)


def _combine_kernel(idx_ref, x_ref, y_ref, o_ref):
    tq = x_ref.shape[0]
    q0 = pl.program_id(2) * tq
    n_exp, cap, td = y_ref.shape
    tokens = q0 + lax.broadcasted_iota(jnp.int32, (tq, 1), 0)
    hit = jnp.where(idx_ref[0] == tokens, 1.0, 0.0).astype(BF16)
    y = y_ref[...].reshape(n_exp * cap, td)
    o_ref[...] = x_ref[...] + jnp.dot(hit, y, preferred_element_type=F32)


def _moe_combine(xt, y, idx, bsz, tq=512, td=512):
    t, d = xt.shape
    n_tok = t // bsz
    n_exp, _, cap = idx.shape[1], None, idx.shape[2]
    idx_flat = idx.reshape(bsz, 1, n_exp * cap)
    tq = min(tq, n_tok)
    td = min(td, d)
    nq = n_tok // tq
    return pl.pallas_call(
        _combine_kernel,
        out_shape=jax.ShapeDtypeStruct((t, d), F32),
        grid=(bsz, d // td, nq),
        in_specs=[pl.BlockSpec((1, 1, n_exp * cap), lambda b, j, q: (b, 0, 0)),
                  pl.BlockSpec((tq, td), lambda b, j, q: (b * nq + q, j)),
                  pl.BlockSpec((n_exp, cap, td), lambda b, j, q: (0, b, j))],
        out_specs=pl.BlockSpec((tq, td), lambda b, j, q: (b * nq + q, j)),
        compiler_params=_cparams("parallel", "parallel", "arbitrary"),
        name="moe_combine",
    )(idx_flat, xt, y)


def _expert_choice_ffn(xt, bsz, n_tok, norm_g, w_router, w1, w3, w2):
    d = xt.shape[1]
    n_exp = w_router.shape[1]
    cap = EC_CAPACITY * n_tok // n_exp
    h2, aff = _norm_router(xt, norm_g.reshape(1, d), w_router)
    aff_t = jnp.swapaxes(aff[:, :n_exp].reshape(bsz, n_tok, n_exp), 1, 2)
    idx, gate = _topk_select(aff_t, cap)
    rows = idx + (jnp.arange(bsz, dtype=jnp.int32) * n_tok)[:, None, None]
    rows = jnp.swapaxes(rows, 0, 1).reshape(n_exp, bsz * cap)
    gate = jnp.swapaxes(gate, 0, 1).reshape(n_exp, bsz * cap, 1)
    xg = _moe_gather(h2.reshape(h2.shape[0], -1, V7X_LANES), rows)
    y = _experts(xg, w1, w3, w2, gate)
    return _moe_combine(xt, y, idx, bsz)


def _final_norm_kernel(x_ref, g_ref, o_ref):
    o_ref[...] = _rms_rows(x_ref[...], g_ref[...], RMS_EPS)


def _final_norm(xt, g, tm=512):
    t, d = xt.shape
    return pl.pallas_call(
        _final_norm_kernel,
        out_shape=jax.ShapeDtypeStruct(xt.shape, xt.dtype),
        grid=(t // tm,),
        in_specs=[pl.BlockSpec((tm, d), lambda i: (i, 0)), pl.BlockSpec((1, d), lambda i: (0, 0))],
        out_specs=pl.BlockSpec((tm, d), lambda i: (i, 0)),
        compiler_params=_cparams("parallel"),
        name="final_norm",
    )(xt, g.reshape(1, d))


def _shift_rows(x, offset):
    n = x.shape[0]
    if offset == 0:
        return x
    row = lax.broadcasted_iota(jnp.int32, x.shape, 0)
    rolled = pltpu.roll(x, (-offset) % n, axis=0)
    valid = (row + offset >= 0) & (row + offset < n)
    return jnp.where(valid, rolled, 0.0)


def _dwconv_rows(x, w_ref, b_ref):
    taps = w_ref.shape[0]
    y = b_ref[...] + jnp.zeros_like(x)
    for kk in range(taps):
        y = y + w_ref[kk:kk + 1, :] * _shift_rows(x, kk - taps // 2)
    return y


def _gelu_tanh(x):
    return 0.5 * x * (1.0 + jnp.tanh(math.sqrt(2.0 / math.pi) * (x + 0.044715 * (x * x * x))))


def _rglru_kernel(x_ref, gate_ref, cw_ref, cb_ref, wa_ref, ba_ref, wx_ref, bx_ref, sp_ref, o_ref,
                  af_scr, bf_scr, ab_scr, bb_scr):
    nb, seq, _ = x_ref.shape
    row = lax.broadcasted_iota(jnp.int32, (seq, x_ref.shape[2]), 0)
    for b in range(nb):
        xc = _dwconv_rows(x_ref[b], cw_ref, cb_ref)
        xcb = xc.astype(BF16)
        for dr, (a_scr, b_scr) in enumerate(((af_scr, bf_scr), (ab_scr, bb_scr))):
            r = jax.nn.sigmoid(jnp.dot(xcb, wa_ref[dr, 0], preferred_element_type=F32) + ba_ref[dr:dr + 1, :])
            i = jax.nn.sigmoid(jnp.dot(xcb, wx_ref[dr, 0], preferred_element_type=F32) + bx_ref[dr:dr + 1, :])
            log_a = -LRU_C * r * sp_ref[dr:dr + 1, :]
            mult = jnp.sqrt(1.0 - jnp.exp(2.0 * log_a))
            mult = jnp.where(row == (seq - 1 if dr else 0), 1.0, mult)
            a_scr[b] = jnp.exp(log_a)
            b_scr[b] = mult * i * xc

    def step(tt, carry):
        new = []
        tb = seq - 1 - tt
        for b in range(nb):
            hf, hb = carry[b]
            hf = af_scr[b, pl.ds(tt, 1), :] * hf + bf_scr[b, pl.ds(tt, 1), :]
            hb = ab_scr[b, pl.ds(tb, 1), :] * hb + bb_scr[b, pl.ds(tb, 1), :]
            bf_scr[b, pl.ds(tt, 1), :] = hf
            bb_scr[b, pl.ds(tb, 1), :] = hb
            new.append((hf, hb))
        return tuple(new)

    zero = jnp.zeros((1, x_ref.shape[2]), F32)
    lax.fori_loop(0, seq, step, tuple((zero, zero) for _ in range(nb)), unroll=8)
    for b in range(nb):
        o_ref[b] = (bf_scr[b] + bb_scr[b]) * _gelu_tanh(gate_ref[b])


def _block_diag_tiles(w, tile):
    nd, nh, hd, _ = w.shape
    per = tile // hd
    w = w.reshape(nd, nh // per, per, hd, hd)
    eye = jnp.eye(per, dtype=w.dtype)
    bd = jnp.einsum('dgpij,pq->dgpiqj', w, eye).reshape(nd, nh // per, tile, tile)
    return bd.astype(BF16)


def _rglru_mixer(proj3, col_x, col_gate, conv_w, conv_b, w_a, b_a, w_x, b_x, lam):
    bsz, seq, _ = proj3.shape
    tc = V7X_LANES
    nct = LRU_W // tc
    sp = jax.nn.softplus(-lam.astype(F32))
    wa = _block_diag_tiles(w_a, tc)
    wx = _block_diag_tiles(w_x, tc)
    vec = lambda i: (0, i)
    return pl.pallas_call(
        _rglru_kernel,
        out_shape=jax.ShapeDtypeStruct((bsz, seq, LRU_W), F32),
        grid=(nct,),
        in_specs=[pl.BlockSpec((bsz, seq, tc), lambda i: (0, 0, col_x // tc + i)),
                  pl.BlockSpec((bsz, seq, tc), lambda i: (0, 0, col_gate // tc + i)),
                  pl.BlockSpec((LRU_CONV, tc), vec),
                  pl.BlockSpec((1, tc), vec),
                  pl.BlockSpec((2, 1, tc, tc), lambda i: (0, i, 0, 0)),
                  pl.BlockSpec((2, tc), vec),
                  pl.BlockSpec((2, 1, tc, tc), lambda i: (0, i, 0, 0)),
                  pl.BlockSpec((2, tc), vec),
                  pl.BlockSpec((2, tc), vec)],
        out_specs=pl.BlockSpec((bsz, seq, tc), lambda i: (0, 0, i)),
        scratch_shapes=[pltpu.VMEM((bsz, seq, tc), F32)] * 4,
        compiler_params=_cparams("parallel"),
        name="rglru",
    )(proj3, proj3, conv_w, conv_b.reshape(1, -1), wa, b_a, wx, b_x, sp)


def _dft_tables(seq):
    n = 3 * seq // 2
    kf = np.arange(n // 2, dtype=np.int64)[:, None]
    s = np.arange(seq, dtype=np.int64)[None, :]
    ang = (np.pi / n) * (((2 * kf + 1) * s) % (2 * n)).astype(np.float64)
    fc = np.cos(ang)
    fs = -np.sin(ang)
    shift = (np.pi / n) * (((2 * kf + 1) * (seq // 2)) % (2 * n)).astype(np.float64)
    to_bf16 = lambda a: jnp.asarray(a.astype(np.float32)).astype(BF16)
    return dict(n=n, fc=to_bf16(fc), fs=to_bf16(fs), fct=to_bf16(fc.T), fst=to_bf16(fs.T),
                pc=jnp.asarray(np.cos(shift).astype(np.float32)), ps=jnp.asarray(np.sin(shift).astype(np.float32)))


def _hyena_pre_kernel(x0_ref, x1_ref, v_ref, w0_ref, w1_ref, w2_ref, b0_ref, b1_ref, b2_ref,
                      vp_ref, vpb_ref, x0c_ref):
    x0c_ref[0] = _dwconv_rows(x0_ref[0], w0_ref, b0_ref)
    vp = _dwconv_rows(v_ref[0], w2_ref, b2_ref) * _dwconv_rows(x1_ref[0], w1_ref, b1_ref)
    vp_ref[0] = vp
    vpb_ref[0] = vp.astype(BF16)


def _hyena_filter_kernel(z_ref, t_ref, dl_ref, w1_ref, b1_ref, w2_ref, b2_ref, w3_ref, b3_ref, w4_ref, fr_ref,
                         fc_ref, fs_ref, pc_ref, ps_ref, hr_ref, hi_ref, filt_scr):
    @pl.when(pl.program_id(0) == 0)
    def _():
        fr = fr_ref[...]
        hid = jnp.sin(fr * (jnp.dot(z_ref[...].astype(BF16), w1_ref[...].astype(BF16),
                                    preferred_element_type=F32) + b1_ref[...]))
        hid = jnp.sin(fr * (jnp.dot(hid.astype(BF16), w2_ref[...].astype(BF16),
                                    preferred_element_type=F32) + b2_ref[...]))
        hid = jnp.sin(fr * (jnp.dot(hid.astype(BF16), w3_ref[...].astype(BF16),
                                    preferred_element_type=F32) + b3_ref[...]))
        filt = jnp.dot(hid.astype(BF16), w4_ref[...].astype(BF16), preferred_element_type=F32)
        filt = filt * jnp.exp(-2.0 * jnp.abs(t_ref[...] - 0.5) * dl_ref[...])
        filt = filt / jnp.sum(jnp.abs(filt), axis=0, keepdims=True)
        filt_scr[...] = filt.astype(BF16)

    f = filt_scr[...]
    hr0 = jnp.dot(fc_ref[...], f, preferred_element_type=F32)
    hi0 = jnp.dot(fs_ref[...], f, preferred_element_type=F32)
    pc = pc_ref[...]
    ps = ps_ref[...]
    hr_ref[...] = hr0 * pc - hi0 * ps
    hi_ref[...] = hr0 * ps + hi0 * pc


def _hyena_fwd_kernel(fc_ref, fs_ref, vp_ref, hr_ref, hi_ref, yr_ref, yi_ref):
    v = vp_ref[0]
    vr = jnp.dot(fc_ref[...], v, preferred_element_type=F32)
    vi = jnp.dot(fs_ref[...], v, preferred_element_type=F32)
    hr = hr_ref[...]
    hi = hi_ref[...]
    yr_ref[0] = (vr * hr - vi * hi).astype(BF16)
    yi_ref[0] = (vr * hi + vi * hr).astype(BF16)


def _hyena_inv_kernel(fct_ref, fst_ref, yr_ref, yi_ref, vp_ref, x0c_ref, bias_ref, o_ref, *, inv_scale):
    y = (jnp.dot(fct_ref[...], yr_ref[0], preferred_element_type=F32)
         + jnp.dot(fst_ref[...], yi_ref[0], preferred_element_type=F32)) * inv_scale
    o_ref[0] = (y + vp_ref[0] * bias_ref[...]) * x0c_ref[0]


def _hyena_mixer_pallas(proj3, col0, conv_w, conv_b, fw1, fb1, fw2, fb2, fw3, fb3, fw4, freq, fft_bias,
                        tk=512, tt=512):
    bsz, seq, _ = proj3.shape
    w = fw4.shape[1]
    tc = V7X_LANES
    nct = w // tc
    cb0 = col0 // tc
    tab = _dft_tables(seq)
    nfreq = tab['n'] // 2
    cwb = lambda off: pl.BlockSpec((HY_SHORT, tc), lambda b, c: (0, off + c))
    cbb = lambda off: pl.BlockSpec((1, tc), lambda b, c: (0, off + c))
    xb = lambda off: pl.BlockSpec((1, seq, tc), lambda b, c: (b, 0, cb0 + off + c))
    ob = pl.BlockSpec((1, seq, tc), lambda b, c: (b, 0, c))
    cb2 = conv_b.reshape(1, -1)
    vp, vpb, x0c = pl.pallas_call(
        _hyena_pre_kernel,
        out_shape=(jax.ShapeDtypeStruct((bsz, seq, w), F32), jax.ShapeDtypeStruct((bsz, seq, w), BF16),
                   jax.ShapeDtypeStruct((bsz, seq, w), F32)),
        grid=(bsz, nct),
        in_specs=[xb(0), xb(nct), xb(2 * nct), cwb(0), cwb(nct), cwb(2 * nct), cbb(0), cbb(nct), cbb(2 * nct)],
        out_specs=(ob, ob, ob),
        compiler_params=_cparams("parallel", "parallel"),
        name="hyena_pre",
    )(proj3, proj3, proj3, conv_w, conv_w, conv_w, cb2, cb2, cb2)

    tcol = np.linspace(0.0, 1.0, seq, dtype=np.float32)[:, None]
    wcol = ((2.0 * math.pi / seq) * np.arange(seq, dtype=np.float32))[:, None]
    bands = np.linspace(1e-4, HY_BANDS - 1, HY_BANDS, dtype=np.float32)[None, :]
    z = np.concatenate([tcol, np.cos(bands * wcol), -np.sin(bands * wcol)], axis=-1).astype(np.float32)
    zp = np.zeros((seq, tc), np.float32)
    zp[:, :HY_EMB] = z
    w1p = jnp.zeros((tc, HY_ORDER), F32).at[:HY_EMB].set(fw1)
    deltas = np.abs(np.linspace(math.log(HY_TARGET) / HY_SLOW, math.log(HY_TARGET) / HY_FAST, w,
                                dtype=np.float32))[None, :]
    full = lambda a: pl.BlockSpec(a.shape, lambda j: (0,) * a.ndim)
    row = lambda a: a.reshape(1, -1)
    small = [jnp.asarray(zp), jnp.asarray(tcol), jnp.asarray(deltas), w1p, row(fb1), fw2, row(fb2), fw3, row(fb3),
             fw4, row(freq)]
    hr, hi = pl.pallas_call(
        _hyena_filter_kernel,
        out_shape=(jax.ShapeDtypeStruct((nfreq, w), F32), jax.ShapeDtypeStruct((nfreq, w), F32)),
        grid=(nfreq // tk,),
        in_specs=[full(a) for a in small]
                 + [pl.BlockSpec((tk, seq), lambda j: (j, 0)), pl.BlockSpec((tk, seq), lambda j: (j, 0)),
                    pl.BlockSpec((tk, 1), lambda j: (j, 0)), pl.BlockSpec((tk, 1), lambda j: (j, 0))],
        out_specs=(pl.BlockSpec((tk, w), lambda j: (j, 0)), pl.BlockSpec((tk, w), lambda j: (j, 0))),
        scratch_shapes=[pltpu.VMEM((seq, w), BF16)],
        compiler_params=_cparams("arbitrary"),
        name="hyena_filter",
    )(*small, tab['fc'], tab['fs'], tab['pc'], tab['ps'])

    yr, yi = pl.pallas_call(
        _hyena_fwd_kernel,
        out_shape=(jax.ShapeDtypeStruct((bsz, nfreq, w), BF16), jax.ShapeDtypeStruct((bsz, nfreq, w), BF16)),
        grid=(bsz, nfreq // tk),
        in_specs=[pl.BlockSpec((tk, seq), lambda b, j: (j, 0)), pl.BlockSpec((tk, seq), lambda b, j: (j, 0)),
                  pl.BlockSpec((1, seq, w), lambda b, j: (b, 0, 0)),
                  pl.BlockSpec((tk, w), lambda b, j: (j, 0)), pl.BlockSpec((tk, w), lambda b, j: (j, 0))],
        out_specs=(pl.BlockSpec((1, tk, w), lambda b, j: (b, j, 0)), pl.BlockSpec((1, tk, w), lambda b, j: (b, j, 0))),
        compiler_params=_cparams("parallel", "parallel"),
        name="hyena_fwd",
    )(tab['fc'], tab['fs'], vpb, hr, hi)

    return pl.pallas_call(
        partial(_hyena_inv_kernel, inv_scale=2.0 / tab['n']),
        out_shape=jax.ShapeDtypeStruct((bsz, seq, w), F32),
        grid=(bsz, seq // tt),
        in_specs=[pl.BlockSpec((tt, nfreq), lambda b, j: (j, 0)), pl.BlockSpec((tt, nfreq), lambda b, j: (j, 0)),
                  pl.BlockSpec((1, nfreq, w), lambda b, j: (b, 0, 0)), pl.BlockSpec((1, nfreq, w), lambda b, j: (b, 0, 0)),
                  pl.BlockSpec((1, tt, w), lambda b, j: (b, j, 0)), pl.BlockSpec((1, tt, w), lambda b, j: (b, j, 0)),
                  pl.BlockSpec((1, w), lambda b, j: (0, 0))],
        out_specs=pl.BlockSpec((1, tt, w), lambda b, j: (b, j, 0)),
        compiler_params=_cparams("parallel", "parallel"),
        name="hyena_inv",
    )(tab['fct'], tab['fst'], yr, yi, vp, x0c, fft_bias.reshape(1, w))


def _split3_bf16(x):
    hi = x.astype(BF16)
    r1 = x - hi.astype(F32)
    mid = r1.astype(BF16)
    lo = (r1 - mid.astype(F32)).astype(BF16)
    return hi, mid, lo


def _exact_tri_matmul(tri, x):
    hi, mid, lo = _split3_bf16(x)
    return (jnp.dot(tri, hi, preferred_element_type=F32) + jnp.dot(tri, mid, preferred_element_type=F32)
            + jnp.dot(tri, lo, preferred_element_type=F32))


def _silu(x):
    return x * jax.nn.sigmoid(x)


def _ssd_kernel(z_ref, xs_ref, bm_ref, cm_ref, dt_ref, cwx_ref, cbx_ref, cwb_ref, cbb_ref, cwc_ref, cbc_ref,
                dtb_ref, alog_ref, dsk_ref, ng_ref, o_ref, xs_scr, bm_scr, cm_scr, dt_scr, da_scr, yb_scr, st_scr,
                *, chunk, head_dim, heads):
    seq = xs_ref.shape[1]
    lanes = V7X_LANES
    n_chunks = seq // chunk
    pairs = heads * head_dim // lanes
    per = lanes // head_dim

    xs_scr[...] = _silu(_dwconv_rows(xs_ref[0], cwx_ref, cbx_ref))
    bm_scr[...] = _silu(_dwconv_rows(bm_ref[0], cwb_ref, cbb_ref))
    cm_scr[...] = _silu(_dwconv_rows(cm_ref[0], cwc_ref, cbc_ref))
    raw = dt_ref[0, 0] + dtb_ref[0]
    dt = jnp.maximum(raw, 0.0) + jnp.log(1.0 + jnp.exp(-jnp.abs(raw)))
    dt_scr[...] = dt
    da_scr[...] = dt * (-jnp.exp(alog_ref[0]))
    st_scr[...] = jnp.zeros_like(st_scr)

    r_i = lax.broadcasted_iota(jnp.int32, (chunk, chunk), 0)
    c_i = lax.broadcasted_iota(jnp.int32, (chunk, chunk), 1)
    tri_lo = (c_i <= r_i).astype(BF16)
    tri_up = (c_i >= r_i).astype(BF16)
    lane = lax.broadcasted_iota(jnp.int32, (1, lanes), 1)
    head_mask = [((lane >= hh * head_dim) & (lane < (hh + 1) * head_dim)).astype(F32) for hh in range(per)]

    def by_head(cols):
        out = cols[0] * head_mask[0]
        for hh in range(1, per):
            out = out + cols[hh] * head_mask[hh]
        return out

    def one_chunk(c, reverse):
        rows = pl.ds(pl.multiple_of(c * chunk, chunk), chunk)
        da = da_scr[rows, :]
        dtc = dt_scr[rows, :]
        cum = _exact_tri_matmul(tri_up if reverse else tri_lo, da)
        cum_t = cum.T
        edge = cum[0:1, :] if reverse else cum[chunk - 1:chunk, :]
        keep = (c_i >= r_i) if reverse else (c_i <= r_i)
        bmat = bm_scr[rows, :]
        cmat = cm_scr[rows, :].astype(BF16)
        cb = lax.dot_general(cmat, bmat.astype(BF16), (((1,), (1,)), ((), ())), preferred_element_type=F32)
        bmat_t = bmat.T.astype(BF16)
        off = heads if reverse else 0
        outs = []
        for p in range(pairs):
            xs = xs_scr[rows, p * lanes:(p + 1) * lanes]
            hs = [off + p * per + hh for hh in range(per)]
            y = None
            for hh, h in enumerate(hs):
                seg = cum[:, h:h + 1] - cum_t[h:h + 1, :]
                lmat = jnp.exp(jnp.where(keep, seg, -1e30))
                xd_h = (xs * dtc[:, h:h + 1] * head_mask[hh]).astype(BF16)
                term = jnp.dot((cb * lmat).astype(BF16), xd_h, preferred_element_type=F32)
                y = term if y is None else y + term
            sidx = (pairs if reverse else 0) + p
            state = st_scr[sidx]
            y = y + (jnp.dot(cmat, state.astype(BF16), preferred_element_type=F32)
                     * by_head([jnp.exp(cum[:, h:h + 1]) for h in hs]))
            xd = xs * by_head([dtc[:, h:h + 1] for h in hs])
            decay_s = by_head([jnp.exp(edge[:, h:h + 1] - cum[:, h:h + 1]) for h in hs])
            chunk_decay = by_head([jnp.exp(edge[:, h:h + 1]) for h in hs])
            st_scr[sidx] = chunk_decay * state + jnp.dot(bmat_t, (xd * decay_s).astype(BF16),
                                                         preferred_element_type=F32)
            outs.append(y)
        return rows, jnp.concatenate(outs, axis=1)

    def body(i, carry):
        rows_f, y_f = one_chunk(i, False)
        o_ref[0, rows_f, :] = y_f
        rows_b, y_b = one_chunk(n_chunks - 1 - i, True)
        yb_scr[rows_b, :] = y_b
        return carry

    lax.fori_loop(0, n_chunks, body, 0)

    y = (o_ref[0] + yb_scr[...] + dsk_ref[...] * xs_scr[...]) * _silu(z_ref[0])
    y = y * lax.rsqrt(jnp.mean(y * y, axis=-1, keepdims=True) + RMS_EPS)
    o_ref[0] = y * ng_ref[...]


def _ssd_mixer_pallas(proj3, col_z, dt_raw, conv_w, conv_b, dt_bias, a_log, d_skip, norm_g):
    bsz, seq, _ = proj3.shape
    lanes = V7X_LANES
    g = SSD_GROUPS
    hg = SSD_HEADS // g
    gw = SSD_W // g
    col_x = col_z + SSD_W
    col_b = col_x + SSD_W
    col_c = col_b + g * SSD_STATE

    def per_group(v):
        lead = v.shape[:-2]
        v = v.reshape(lead + (2, g, hg))
        v = jnp.moveaxis(v, -2, 0).reshape((g,) + lead + (2 * hg,))
        return jnp.pad(v, [(0, 0)] * (v.ndim - 1) + [(0, lanes - 2 * hg)])
    dtg = jnp.moveaxis(per_group(dt_raw[..., :2 * SSD_HEADS].reshape(bsz, seq, 2, SSD_HEADS)), 0, 1)
    dtb = per_group(dt_bias.astype(F32)).reshape(g, 1, lanes)
    alog = per_group(a_log.astype(F32)).reshape(g, 1, lanes)
    dsk = jnp.repeat(d_skip.astype(F32), SSD_HD).reshape(1, SSD_W)
    cb2 = conv_b.reshape(1, -1)
    nsb = SSD_W // SSD_STATE
    kern = partial(_ssd_kernel, chunk=SSD_CHUNK, head_dim=SSD_HD, heads=hg)
    return pl.pallas_call(
        kern,
        out_shape=jax.ShapeDtypeStruct((bsz, seq, SSD_W), F32),
        grid=(bsz, g),
        in_specs=[pl.BlockSpec((1, seq, gw), lambda b, gi: (b, 0, col_z // gw + gi)),
                  pl.BlockSpec((1, seq, gw), lambda b, gi: (b, 0, col_x // gw + gi)),
                  pl.BlockSpec((1, seq, SSD_STATE), lambda b, gi: (b, 0, col_b // SSD_STATE + gi)),
                  pl.BlockSpec((1, seq, SSD_STATE), lambda b, gi: (b, 0, col_c // SSD_STATE + gi)),
                  pl.BlockSpec((1, 1, seq, lanes), lambda b, gi: (b, gi, 0, 0)),
                  pl.BlockSpec((SSD_CONV, gw), lambda b, gi: (0, gi)),
                  pl.BlockSpec((1, gw), lambda b, gi: (0, gi)),
                  pl.BlockSpec((SSD_CONV, SSD_STATE), lambda b, gi: (0, nsb + gi)),
                  pl.BlockSpec((1, SSD_STATE), lambda b, gi: (0, nsb + gi)),
                  pl.BlockSpec((SSD_CONV, SSD_STATE), lambda b, gi: (0, nsb + g + gi)),
                  pl.BlockSpec((1, SSD_STATE), lambda b, gi: (0, nsb + g + gi)),
                  pl.BlockSpec((1, 1, lanes), lambda b, gi: (gi, 0, 0)),
                  pl.BlockSpec((1, 1, lanes), lambda b, gi: (gi, 0, 0)),
                  pl.BlockSpec((1, gw), lambda b, gi: (0, gi)),
                  pl.BlockSpec((1, gw), lambda b, gi: (0, gi))],
        out_specs=pl.BlockSpec((1, seq, gw), lambda b, gi: (b, 0, gi)),
        scratch_shapes=[pltpu.VMEM((seq, gw), F32), pltpu.VMEM((seq, SSD_STATE), F32),
                        pltpu.VMEM((seq, SSD_STATE), F32), pltpu.VMEM((seq, lanes), F32),
                        pltpu.VMEM((seq, lanes), F32), pltpu.VMEM((seq, gw), F32),
                        pltpu.VMEM((2 * gw // lanes, SSD_STATE, lanes), F32)],
        compiler_params=_cparams("parallel", "parallel"),
        name="ssd",
    )(proj3, proj3, proj3, proj3, dtg, conv_w, cb2, conv_w, cb2, conv_w, cb2, dtb, alog, dsk,
      norm_g.reshape(1, SSD_W))


def _t5_bucket(rel):
    nb = REL_BUCKETS // 2
    ret = (rel > 0).astype(jnp.int32) * nb
    n = jnp.abs(rel)
    max_exact = nb // 2
    large = max_exact + (jnp.log(jnp.maximum(n, 1).astype(jnp.float32) / max_exact)
                         / math.log(REL_MAX_DIST / max_exact) * (nb - max_exact)).astype(jnp.int32)
    large = jnp.minimum(large, nb - 1)
    return ret + jnp.where(n < max_exact, n, large)


MIX_COLS = HY_COLS + LRU_COLS + DA_COLS + SSD_W + SSD_XBC
DT_COLS = 2 * SSD_HEADS
MIX_PAD = MIX_COLS + V7X_LANES


def kernel(x, norm1_g, w_in, hy_conv_w, hy_conv_b, hy_fw1, hy_fb1, hy_fw2, hy_fb2, hy_fw3, hy_fb3,
           hy_fw4, hy_freq, hy_bias, lru_conv_w, lru_conv_b, lru_wa, lru_ba, lru_wx, lru_bx, lru_lam,
           da_lam, da_subln_g, ssd_conv_w, ssd_conv_b, ssd_dt_bias, ssd_a_log, ssd_d, ssd_norm_g,
           w_branch, b_gate, w_out, norm2_g, w_router, moe_w1, moe_w3, moe_w2, rel_bias, final_g):
    bsz, seq_len, d_model = x.shape
    t = bsz * seq_len
    xt = x.reshape(t, d_model)
    o1 = HY_COLS
    o2 = o1 + LRU_COLS
    o3 = o2 + DA_COLS
    o4 = o3 + SSD_W + SSD_XBC
    for l in range(DEPTH):
        g1 = norm1_g[l].reshape(1, d_model)
        wq = _cast_layer_weight(w_in, l)
        proj, dt_raw = _inproj(xt, g1, wq, MIX_COLS)
        p3 = proj.reshape(bsz, seq_len, MIX_COLS)
        o_hy = _hyena_mixer_pallas(p3, 0, hy_conv_w[l], hy_conv_b[l], hy_fw1[l], hy_fb1[l], hy_fw2[l],
                                   hy_fb2[l], hy_fw3[l], hy_fb3[l], hy_fw4[l], hy_freq[l], hy_bias[l])
        o_lru = _rglru_mixer(p3, o1, o1 + LRU_W, lru_conv_w[l], lru_conv_b[l],
                             lru_wa[l], lru_ba[l], lru_wx[l], lru_bx[l], lru_lam[l])
        lam_init = 0.8 - 0.6 * math.exp(-0.3 * l)
        o_da = _diff_attention(proj, o2, bsz, seq_len, da_lam[l], da_subln_g[l], rel_bias, lam_init)
        o_ssd = _ssd_mixer_pallas(p3, o3, dt_raw.reshape(bsz, seq_len, V7X_LANES), ssd_conv_w[l], ssd_conv_b[l],
                                  ssd_dt_bias[l], ssd_a_log[l], ssd_d[l], ssd_norm_g[l])
        branches = [o_hy.reshape(t, BR_W), o_lru.reshape(t, BR_W), o_da, o_ssd.reshape(t, BR_W)]
        xt = _gated_merge(xt, g1, wq, MIX_COLS + DT_COLS, b_gate[l], branches, w_branch[l].astype(BF16),
                          w_out[l].astype(BF16))
        xt = _expert_choice_ffn(xt, bsz, seq_len, norm2_g[l], w_router[l], moe_w1[l], moe_w3[l], moe_w2[l])
    return _final_norm(xt, final_g).reshape(bsz, seq_len, d_model)
```

```python
import math
from functools import partial

import numpy as np
import jax
import jax.numpy as jnp
from jax import lax
from jax.experimental import pallas as pl
from jax.experimental.pallas import tpu as pltpu

D_MODEL = 2048
BATCH = 4
SEQ = 2048
DEPTH = 2

N_BRANCH = 4
BR_W = D_MODEL // 4
RMS_EPS = 1e-6

HY_W = BR_W
HY_SHORT = 3
HY_EMB = 33
HY_BANDS = (HY_EMB - 1) // 2
HY_ORDER = 64
HY_TARGET = 1e-2
HY_FAST = 0.3
HY_SLOW = 1.5

LRU_W = BR_W
LRU_HEADS = 8
LRU_HD = LRU_W // LRU_HEADS
LRU_CONV = 4
LRU_C = 8.0

DA_HEADS = 4
DA_HD = BR_W // (2 * DA_HEADS)
DA_QBLOCK = 128
REL_BUCKETS = 32
REL_MAX_DIST = 128

SSD_W = BR_W
SSD_HD = 64
SSD_HEADS = SSD_W // SSD_HD
SSD_GROUPS = 2
SSD_STATE = 128
SSD_CONV = 4
SSD_CHUNK = 128

N_EXPERTS = 16
EC_CAPACITY = 2
D_EXPERT = D_MODEL

HY_COLS = 3 * HY_W
LRU_COLS = 2 * LRU_W
DA_COLS = 3 * DA_HEADS * 2 * DA_HD
SSD_XBC = SSD_W + 2 * SSD_GROUPS * SSD_STATE
SSD_COLS = SSD_W + SSD_XBC + 2 * SSD_HEADS
GATE_COLS = N_BRANCH * D_MODEL
IN_COLS = HY_COLS + LRU_COLS + DA_COLS + SSD_COLS + GATE_COLS

V7X_LANES = 128
V7X_VMEM_BYTES = 64 * 1024 * 1024
VMEM_LIMIT_BYTES = V7X_VMEM_BYTES - 8 * 1024 * 1024

BF16 = jnp.bfloat16
F32 = jnp.float32


def _cparams(*sem):
    return pltpu.CompilerParams(dimension_semantics=sem, vmem_limit_bytes=VMEM_LIMIT_BYTES)


def _rms_rows(x, g, eps):
    ms = jnp.mean(x * x, axis=-1, keepdims=True)
    return x * lax.rsqrt(ms + eps) * g


def _inproj_kernel(x_ref, g_ref, w_ref, ws_ref, o_ref, os_ref, h_scr):
    @pl.when(pl.program_id(1) == 0)
    def _():
        h_scr[...] = _rms_rows(x_ref[...], g_ref[...], RMS_EPS).astype(BF16)
        os_ref[...] = jnp.dot(h_scr[...], ws_ref[...], preferred_element_type=F32)

    o_ref[...] = jnp.dot(h_scr[...], w_ref[...], preferred_element_type=F32)


def _inproj(xt, g, wq, n, tm=1024, tn=512):
    t, d = xt.shape
    ns = V7X_LANES
    return pl.pallas_call(
        _inproj_kernel,
        out_shape=(jax.ShapeDtypeStruct((t, n), F32), jax.ShapeDtypeStruct((t, ns), F32)),
        grid=(t // tm, n // tn),
        in_specs=[pl.BlockSpec((tm, d), lambda i, j: (i, 0)),
                  pl.BlockSpec((1, d), lambda i, j: (0, 0)),
                  pl.BlockSpec((d, tn), lambda i, j: (0, j)),
                  pl.BlockSpec((d, ns), lambda i, j: (0, n // ns))],
        out_specs=(pl.BlockSpec((tm, tn), lambda i, j: (i, j)), pl.BlockSpec((tm, ns), lambda i, j: (i, 0))),
        scratch_shapes=[pltpu.VMEM((tm, d), BF16)],
        compiler_params=_cparams("parallel", "arbitrary"),
        name="inproj",
    )(xt, g, wq, wq)


def _diffattn_kernel(relb_ref, q_ref, k_ref, v_ref, bucket_ref, lamqk_ref, g_ref, o_ref, *,
                     tq, lam_init, head_dim):
    h = pl.program_id(1)
    seq = q_ref.shape[0]
    margin = REL_MAX_DIST
    bucket = bucket_ref[...]
    band = jnp.zeros(bucket.shape, F32)
    for b in range(REL_BUCKETS):
        band = jnp.where(bucket == b, relb_ref[h, b], band)
    c_neg = band[0:1, 0:1]
    c_pos = band[tq - 1:tq, tq + 2 * margin - 1:tq + 2 * margin]

    lq = lamqk_ref[...]
    lam = (jnp.exp(jnp.sum(lq[0:1] * lq[1:2], axis=-1, keepdims=True))
           - jnp.exp(jnp.sum(lq[2:3] * lq[3:4], axis=-1, keepdims=True)) + lam_init)

    lane = lax.broadcasted_iota(jnp.int32, (1, 2 * head_dim), 1)
    lo_mask = (lane < head_dim).astype(F32)
    hi_mask = 1.0 - lo_mask
    k = k_ref[...].astype(BF16)
    v = v_ref[...].astype(BF16)
    scale = head_dim ** -0.5
    dn = (((1,), (1,)), ((), ()))
    for i in range(seq // tq):
        q0 = i * tq
        q = q_ref[q0:q0 + tq, :] * scale
        q1 = (q * lo_mask).astype(BF16)
        q2 = (q * hi_mask).astype(BF16)
        lo = max(q0 - margin, 0)
        hi = min(q0 + tq + margin, seq)
        pieces = []
        if lo > 0:
            pieces.append(jnp.broadcast_to(c_neg, (tq, lo)))
        pieces.append(band[:, lo - (q0 - margin):hi - (q0 - margin)])
        if hi < seq:
            pieces.append(jnp.broadcast_to(c_pos, (tq, seq - hi)))
        bias = jnp.concatenate(pieces, axis=1) if len(pieces) > 1 else pieces[0]
        s1 = lax.dot_general(q1, k, dn, preferred_element_type=F32) + bias
        s2 = lax.dot_general(q2, k, dn, preferred_element_type=F32) + bias
        p1 = jnp.exp(s1 - jnp.max(s1, axis=-1, keepdims=True))
        p2 = jnp.exp(s2 - jnp.max(s2, axis=-1, keepdims=True))
        r1 = 1.0 / jnp.sum(p1, axis=-1, keepdims=True)
        r2 = lam / jnp.sum(p2, axis=-1, keepdims=True)
        a = p1 * r1 - p2 * r2
        o = jnp.dot(a.astype(BF16), v, preferred_element_type=F32)
        o = _rms_rows(o, g_ref[...], 1e-5) * (1.0 - lam_init)
        o_ref[q0:q0 + tq, :] = o


def _diff_attention(proj, col0, bsz, seq, lam_qk, subln_g, rel_bias, lam_init, tq=256):
    hw = 2 * DA_HD
    cb = col0 // hw
    margin = REL_MAX_DIST
    r = jnp.arange(tq)[:, None]
    c = jnp.arange(tq + 2 * margin)[None, :]
    bucket = _t5_bucket(c - margin - r)
    kern = partial(_diffattn_kernel, tq=tq, lam_init=lam_init, head_dim=DA_HD)
    return pl.pallas_call(
        kern,
        out_shape=jax.ShapeDtypeStruct((bsz * seq, DA_HEADS * hw), F32),
        grid=(bsz, DA_HEADS),
        in_specs=[pl.BlockSpec(memory_space=pltpu.SMEM),
                  pl.BlockSpec((seq, hw), lambda b, h: (b, cb + h)),
                  pl.BlockSpec((seq, hw), lambda b, h: (b, cb + DA_HEADS + h)),
                  pl.BlockSpec((seq, hw), lambda b, h: (b, cb + 2 * DA_HEADS + h)),
                  pl.BlockSpec(bucket.shape, lambda b, h: (0, 0)),
                  pl.BlockSpec(lam_qk.shape, lambda b, h: (0, 0)),
                  pl.BlockSpec((1, hw), lambda b, h: (0, 0))],
        out_specs=pl.BlockSpec((seq, hw), lambda b, h: (b, h)),
        compiler_params=_cparams("parallel", "parallel"),
        name="diffattn",
    )(rel_bias.T, proj, proj, proj, bucket, lam_qk, subln_g.reshape(1, hw))


def _merge_kernel(x_ref, g_ref, wg0_ref, wg1_ref, wg2_ref, wg3_ref, bg_ref, b0_ref, b1_ref, b2_ref, b3_ref,
                  wb_ref, wo_ref, o_ref, h_scr, br_scr, acc_scr, *, shift):
    j = pl.program_id(1)
    wg_ref = (wg0_ref, wg1_ref, wg2_ref, wg3_ref)

    @pl.when(j == 0)
    def _():
        h_scr[...] = _rms_rows(x_ref[...], g_ref[...], RMS_EPS).astype(BF16)
        for kk, b_ref in enumerate((b0_ref, b1_ref, b2_ref, b3_ref)):
            br_scr[kk] = b_ref[...].astype(BF16)
        acc_scr[...] = jnp.zeros_like(acc_scr)

    h = h_scr[...]
    tc = wo_ref.shape[0]
    c_pos = j * tc + lax.broadcasted_iota(jnp.int32, (1, tc), 1)
    valid = (c_pos >= shift) & (c_pos < shift + x_ref.shape[1])
    m = None
    for kk in range(N_BRANCH):
        gate = jnp.dot(h, wg_ref[kk][...], preferred_element_type=F32) + bg_ref[kk:kk + 1, :]
        bp = jnp.dot(br_scr[kk], wb_ref[kk], preferred_element_type=F32)
        term = jax.nn.sigmoid(gate) * bp
        m = term if m is None else m + term
    m = jnp.where(valid, m, 0.0)
    acc_scr[...] += jnp.dot(m.astype(BF16), wo_ref[...], preferred_element_type=F32)

    @pl.when(j == pl.num_programs(1) - 1)
    def _():
        o_ref[...] = x_ref[...] + acc_scr[...]


def _gated_merge(xt, g, wq, gate_col0, bg, branches, wb, wo, tm=512, tc=256):
    t, d = xt.shape
    bw = branches[0].shape[1]
    shift = gate_col0 % tc
    blk0 = gate_col0 // tc
    ncb = d // tc + 1
    dp = ncb * tc
    bg_s = jnp.pad(bg, ((0, 0), (shift, dp - d - shift)))
    wb_s = jnp.pad(wb, ((0, 0), (0, 0), (shift, dp - d - shift)))
    wo_s = jnp.pad(wo, ((shift, dp - d - shift), (0, 0)))
    return pl.pallas_call(
        partial(_merge_kernel, shift=shift),
        out_shape=jax.ShapeDtypeStruct((t, d), F32),
        grid=(t // tm, ncb),
        in_specs=[pl.BlockSpec((tm, d), lambda i, j: (i, 0)),
                  pl.BlockSpec((1, d), lambda i, j: (0, 0))]
                 + [pl.BlockSpec((d, tc), lambda i, j, kk=kk: (0, blk0 + kk * (d // tc) + j))
                    for kk in range(N_BRANCH)]
                 + [pl.BlockSpec((N_BRANCH, tc), lambda i, j: (0, j))]
                 + [pl.BlockSpec((tm, bw), lambda i, j: (i, 0))] * N_BRANCH
                 + [pl.BlockSpec((N_BRANCH, bw, tc), lambda i, j: (0, 0, j)),
                    pl.BlockSpec((tc, d), lambda i, j: (j, 0))],
        out_specs=pl.BlockSpec((tm, d), lambda i, j: (i, 0)),
        scratch_shapes=[pltpu.VMEM((tm, d), BF16), pltpu.VMEM((N_BRANCH, tm, bw), BF16),
                        pltpu.VMEM((tm, d), F32)],
        compiler_params=_cparams("parallel", "arbitrary"),
        name="gated_merge",
    )(xt, g, wq, wq, wq, wq, bg_s, *branches, wb_s, wo_s)


def _pack_bf16_halves(h):
    half = h.shape[1] // 2
    bits = lax.bitcast_convert_type(h.astype(F32), jnp.uint32)
    return (bits[:, :half] >> 16) | (bits[:, half:] & jnp.uint32(0xFFFF0000))


def _unpack_bf16_halves(p):
    lo = lax.bitcast_convert_type(p << 16, F32).astype(BF16)
    hi = lax.bitcast_convert_type(p & jnp.uint32(0xFFFF0000), F32).astype(BF16)
    return lo, hi


def _router_kernel(x_ref, g_ref, wr_ref, h_ref, aff_ref, *, n_experts):
    h = _rms_rows(x_ref[...], g_ref[...], RMS_EPS).astype(BF16)
    h_ref[...] = _pack_bf16_halves(h)
    logits = jnp.dot(h, wr_ref[...], preferred_element_type=F32)
    lane = lax.broadcasted_iota(jnp.int32, logits.shape, 1)
    logits = jnp.where(lane < n_experts, logits, -jnp.inf)
    p = jnp.exp(logits - jnp.max(logits, axis=-1, keepdims=True))
    aff_ref[...] = p / jnp.sum(p, axis=-1, keepdims=True)


def _norm_router(xt, g, w_router, tm=512):
    t, d = xt.shape
    e = w_router.shape[1]
    wr = jnp.zeros((d, V7X_LANES), BF16).at[:, :e].set(w_router.astype(BF16))
    return pl.pallas_call(
        partial(_router_kernel, n_experts=e),
        out_shape=(jax.ShapeDtypeStruct((t, d // 2), jnp.uint32), jax.ShapeDtypeStruct((t, V7X_LANES), F32)),
        grid=(t // tm,),
        in_specs=[pl.BlockSpec((tm, d), lambda i: (i, 0)),
                  pl.BlockSpec((1, d), lambda i: (0, 0)),
                  pl.BlockSpec((d, V7X_LANES), lambda i: (0, 0))],
        out_specs=(pl.BlockSpec((tm, d // 2), lambda i: (i, 0)),
                   pl.BlockSpec((tm, V7X_LANES), lambda i: (i, 0))),
        compiler_params=_cparams("parallel"),
        name="norm_router",
    )(xt, g, wr)


def _expert_kernel(xg_ref, w1_ref, w3_ref, w2_ref, gate_ref, o_ref, x_scr, acc_scr):
    f = pl.program_id(1)
    half = x_scr.shape[1] // 2

    @pl.when(f == 0)
    def _():
        c = x_scr.shape[0]
        lanes = xg_ref.shape[2]
        sub = xg_ref.shape[1] // c
        for s in range(sub):
            lo, hi = _unpack_bf16_halves(xg_ref[0, pl.ds(s, c, stride=sub), :])
            x_scr[:, s * lanes:(s + 1) * lanes] = lo
            x_scr[:, half + s * lanes:half + (s + 1) * lanes] = hi
        acc_scr[...] = jnp.zeros_like(acc_scr)

    xg = x_scr[...]
    a = jnp.dot(xg, w1_ref[0, 0].astype(BF16), preferred_element_type=F32)
    b = jnp.dot(xg, w3_ref[0, 0].astype(BF16), preferred_element_type=F32)
    hid = (a * jax.nn.sigmoid(a) * b).astype(BF16)
    acc_scr[...] += jnp.dot(hid, w2_ref[0, 0].astype(BF16), preferred_element_type=F32)

    @pl.when(f == pl.num_programs(1) - 1)
    def _():
        o_ref[0] = (acc_scr[...] * gate_ref[0]).astype(BF16)


def _experts(xg, w1, w3, w2, layer, gate, tf=256):
    e, rows, lanes = xg.shape
    c = gate.shape[1]
    d = w1.shape[2]
    f = w1.shape[3]
    return pl.pallas_call(
        _expert_kernel,
        out_shape=jax.ShapeDtypeStruct((e, c, d), BF16),
        grid=(e, f // tf),
        in_specs=[pl.BlockSpec((1, rows, lanes), lambda i, j: (i, 0, 0)),
                  pl.BlockSpec((1, 1, d, tf), lambda i, j: (layer, i, 0, j)),
                  pl.BlockSpec((1, 1, d, tf), lambda i, j: (layer, i, 0, j)),
                  pl.BlockSpec((1, 1, tf, d), lambda i, j: (layer, i, j, 0)),
                  pl.BlockSpec((1, c, 1), lambda i, j: (i, 0, 0))],
        out_specs=pl.BlockSpec((1, c, d), lambda i, j: (i, 0, 0)),
        scratch_shapes=[pltpu.VMEM((c, d), BF16), pltpu.VMEM((c, d), F32)],
        compiler_params=_cparams("parallel", "arbitrary"),
        name="experts",
    )(xg, w1, w3, w2, gate)


def _prefix_count(mask_f, tri):
    rows, n = mask_f.shape
    w = tri.shape[0]
    run = jnp.zeros((rows, 1), F32)
    outs = []
    for c in range(n // w):
        blk = mask_f[:, c * w:(c + 1) * w]
        outs.append(jnp.dot(blk.astype(BF16), tri, preferred_element_type=F32) + run)
        run = run + jnp.sum(blk, axis=-1, keepdims=True)
    return jnp.concatenate(outs, axis=1), run


def _topk_kernel(aff_ref, idx_ref, gate_ref, *, cap):
    aff = aff_ref[0]
    n_exp, n_tok = aff.shape
    keys = lax.bitcast_convert_type(aff, jnp.int32)
    thr = jnp.zeros((n_exp, 1), jnp.int32)
    for bit in range(30, -1, -1):
        cand = thr | (1 << bit)
        cnt = jnp.sum((keys >= cand).astype(F32), axis=-1, keepdims=True)
        thr = jnp.where(cnt >= cap, cand, thr)
    gt = (keys > thr).astype(F32)
    eq = (keys == thr).astype(F32)
    w = V7X_LANES
    r_i = lax.broadcasted_iota(jnp.int32, (w, w), 0)
    c_i = lax.broadcasted_iota(jnp.int32, (w, w), 1)
    tri = (r_i < c_i).astype(BF16)
    need = cap - jnp.sum(gt, axis=-1, keepdims=True)
    eq_rank, _ = _prefix_count(eq, tri)
    sel = gt + eq * (eq_rank < need).astype(F32)
    pos, _ = _prefix_count(sel, tri)
    tok = lax.broadcasted_iota(jnp.int32, (1, n_tok), 1)
    a_h = aff.astype(BF16)
    rem = aff - a_h.astype(F32)
    a_m = rem.astype(BF16)
    a_l = (rem - a_m.astype(F32)).astype(BF16)
    slot = lax.broadcasted_iota(jnp.int32, (cap, n_tok), 0).astype(F32)
    pos = jnp.where(sel > 0.5, pos, -1.0)
    dn = (((1,), (1,)), ((), ()))
    for e in range(n_exp):
        hit = jnp.where(pos[e:e + 1, :] == slot, 1.0, 0.0).astype(BF16)
        src = jnp.concatenate([(tok >> 6).astype(F32), (tok & 63).astype(F32),
                               a_h[e:e + 1, :].astype(F32), a_m[e:e + 1, :].astype(F32),
                               a_l[e:e + 1, :].astype(F32), jnp.zeros((3, n_tok), F32)], axis=0).astype(BF16)
        res = lax.dot_general(src, hit, dn, preferred_element_type=F32)
        idx_ref[0, e:e + 1, :] = (res[0:1] * 64.0 + res[1:2]).astype(jnp.int32)
        gate_ref[0, e:e + 1, :] = res[2:3] + res[3:4] + res[4:5]


def _topk_select(aff_t, cap):
    bsz, n_exp, n_tok = aff_t.shape
    blk = lambda n: pl.BlockSpec((1, n_exp, n), lambda b: (b, 0, 0))
    return pl.pallas_call(
        partial(_topk_kernel, cap=cap),
        out_shape=(jax.ShapeDtypeStruct((bsz, n_exp, cap), jnp.int32),
                   jax.ShapeDtypeStruct((bsz, n_exp, cap), F32)),
        grid=(bsz,),
        in_specs=[blk(n_tok)],
        out_specs=(blk(cap), blk(cap)),
        compiler_params=_cparams("parallel"),
        name="topk_select",
    )(aff_t)


def _gather_kernel(rows_ref, h_hbm, o_ref, sem):
    e = pl.program_id(0)
    sub = h_hbm.shape[1]
    n = o_ref.shape[1] // sub

    def start(j, carry):
        dst = o_ref.at[0, pl.ds(pl.multiple_of(j * sub, sub), sub)]
        pltpu.make_async_copy(h_hbm.at[rows_ref[e, j]], dst, sem).start()
        return carry

    lax.fori_loop(0, n, start, 0, unroll=8)
    pltpu.make_async_copy(o_ref.at[0], o_ref.at[0], sem).wait()


def _moe_gather(h, rows):
    n_exp, c = rows.shape
    _, sub, lanes = h.shape
    return pl.pallas_call(
        _gather_kernel,
        out_shape=jax.ShapeDtypeStruct((n_exp, c * sub, lanes), h.dtype),
        grid_spec=pltpu.PrefetchScalarGridSpec(
            num_scalar_prefetch=1,
            grid=(n_exp,),
            in_specs=[pl.BlockSpec(memory_space=pl.ANY)],
            out_specs=pl.BlockSpec((1, c * sub, lanes), lambda e, rows: (e, 0, 0)),
            scratch_shapes=[pltpu.SemaphoreType.DMA(())]),
        compiler_params=_cparams("arbitrary"),
        name="moe_gather",
    )(rows, h)


def _combine_kernel(idx_ref, x_ref, y_ref, o_ref):
    tq = x_ref.shape[0]
    q0 = pl.program_id(2) * tq
    n_exp, cap, td = y_ref.shape
    tokens = q0 + lax.broadcasted_iota(jnp.int32, (tq, 1), 0)
    hit = jnp.where(idx_ref[0] == tokens, 1.0, 0.0).astype(BF16)
    y = y_ref[...].reshape(n_exp * cap, td)
    o_ref[...] = x_ref[...] + jnp.dot(hit, y, preferred_element_type=F32)


def _moe_combine(xt, y, idx, bsz, tq=512, td=512):
    t, d = xt.shape
    n_tok = t // bsz
    n_exp, _, cap = idx.shape[1], None, idx.shape[2]
    idx_flat = idx.reshape(bsz, 1, n_exp * cap)
    tq = min(tq, n_tok)
    td = min(td, d)
    nq = n_tok // tq
    return pl.pallas_call(
        _combine_kernel,
        out_shape=jax.ShapeDtypeStruct((t, d), F32),
        grid=(bsz, d // td, nq),
        in_specs=[pl.BlockSpec((1, 1, n_exp * cap), lambda b, j, q: (b, 0, 0)),
                  pl.BlockSpec((tq, td), lambda b, j, q: (b * nq + q, j)),
                  pl.BlockSpec((n_exp, cap, td), lambda b, j, q: (0, b, j))],
        out_specs=pl.BlockSpec((tq, td), lambda b, j, q: (b * nq + q, j)),
        compiler_params=_cparams("parallel", "parallel", "arbitrary"),
        name="moe_combine",
    )(idx_flat, xt, y)


def _expert_choice_ffn(xt, bsz, n_tok, norm_g, w_router, w1, w3, w2, layer):
    d = xt.shape[1]
    n_exp = w_router.shape[1]
    cap = EC_CAPACITY * n_tok // n_exp
    h2, aff = _norm_router(xt, norm_g.reshape(1, d), w_router)
    aff_t = jnp.swapaxes(aff[:, :n_exp].reshape(bsz, n_tok, n_exp), 1, 2)
    idx, gate = _topk_select(aff_t, cap)
    rows = idx + (jnp.arange(bsz, dtype=jnp.int32) * n_tok)[:, None, None]
    rows = jnp.swapaxes(rows, 0, 1).reshape(n_exp, bsz * cap)
    gate = jnp.swapaxes(gate, 0, 1).reshape(n_exp, bsz * cap, 1)
    xg = _moe_gather(h2.reshape(h2.shape[0], -1, V7X_LANES), rows)
    y = _experts(xg, w1, w3, w2, layer, gate)
    return _moe_combine(xt, y, idx, bsz)


def _final_norm_kernel(x_ref, g_ref, o_ref):
    o_ref[...] = _rms_rows(x_ref[...], g_ref[...], RMS_EPS)


def _final_norm(xt, g, tm=512):
    t, d = xt.shape
    return pl.pallas_call(
        _final_norm_kernel,
        out_shape=jax.ShapeDtypeStruct(xt.shape, xt.dtype),
        grid=(t // tm,),
        in_specs=[pl.BlockSpec((tm, d), lambda i: (i, 0)), pl.BlockSpec((1, d), lambda i: (0, 0))],
        out_specs=pl.BlockSpec((tm, d), lambda i: (i, 0)),
        compiler_params=_cparams("parallel"),
        name="final_norm",
    )(xt, g.reshape(1, d))


def _shift_rows(x, offset):
    n = x.shape[0]
    if offset == 0:
        return x
    row = lax.broadcasted_iota(jnp.int32, x.shape, 0)
    rolled = pltpu.roll(x, (-offset) % n, axis=0)
    valid = (row + offset >= 0) & (row + offset < n)
    return jnp.where(valid, rolled, 0.0)


def _dwconv_rows(x, w_ref, b_ref):
    taps = w_ref.shape[0]
    y = b_ref[...] + jnp.zeros_like(x)
    for kk in range(taps):
        y = y + w_ref[kk:kk + 1, :] * _shift_rows(x, kk - taps // 2)
    return y


def _gelu_tanh(x):
    return 0.5 * x * (1.0 + jnp.tanh(math.sqrt(2.0 / math.pi) * (x + 0.044715 * (x * x * x))))


def _rglru_kernel(x_ref, gate_ref, cw_ref, cb_ref, wa_ref, ba_ref, wx_ref, bx_ref, sp_ref, o_ref,
                  af_scr, bf_scr, ab_scr, bb_scr):
    nb, seq, _ = x_ref.shape
    row = lax.broadcasted_iota(jnp.int32, (seq, x_ref.shape[2]), 0)
    for b in range(nb):
        xc = _dwconv_rows(x_ref[b], cw_ref, cb_ref)
        xcb = xc.astype(BF16)
        for dr, (a_scr, b_scr) in enumerate(((af_scr, bf_scr), (ab_scr, bb_scr))):
            r = jax.nn.sigmoid(jnp.dot(xcb, wa_ref[dr, 0], preferred_element_type=F32) + ba_ref[dr:dr + 1, :])
            i = jax.nn.sigmoid(jnp.dot(xcb, wx_ref[dr, 0], preferred_element_type=F32) + bx_ref[dr:dr + 1, :])
            log_a = -LRU_C * r * sp_ref[dr:dr + 1, :]
            mult = jnp.sqrt(1.0 - jnp.exp(2.0 * log_a))
            mult = jnp.where(row == (seq - 1 if dr else 0), 1.0, mult)
            a_scr[b] = jnp.exp(log_a)
            b_scr[b] = mult * i * xc

    def step(tt, carry):
        new = []
        tb = seq - 1 - tt
        for b in range(nb):
            hf, hb = carry[b]
            hf = af_scr[b, pl.ds(tt, 1), :] * hf + bf_scr[b, pl.ds(tt, 1), :]
            hb = ab_scr[b, pl.ds(tb, 1), :] * hb + bb_scr[b, pl.ds(tb, 1), :]
            bf_scr[b, pl.ds(tt, 1), :] = hf
            bb_scr[b, pl.ds(tb, 1), :] = hb
            new.append((hf, hb))
        return tuple(new)

    zero = jnp.zeros((1, x_ref.shape[2]), F32)
    lax.fori_loop(0, seq, step, tuple((zero, zero) for _ in range(nb)), unroll=8)
    for b in range(nb):
        o_ref[b] = (bf_scr[b] + bb_scr[b]) * _gelu_tanh(gate_ref[b])


def _block_diag_tiles(w, tile):
    nd, nh, hd, _ = w.shape
    per = tile // hd
    w = w.reshape(nd, nh // per, per, hd, hd)
    eye = jnp.eye(per, dtype=w.dtype)
    bd = jnp.einsum('dgpij,pq->dgpiqj', w, eye).reshape(nd, nh // per, tile, tile)
    return bd.astype(BF16)


def _rglru_mixer(proj3, col_x, col_gate, conv_w, conv_b, w_a, b_a, w_x, b_x, lam):
    bsz, seq, _ = proj3.shape
    tc = V7X_LANES
    nct = LRU_W // tc
    sp = jax.nn.softplus(-lam.astype(F32))
    wa = _block_diag_tiles(w_a, tc)
    wx = _block_diag_tiles(w_x, tc)
    vec = lambda i: (0, i)
    return pl.pallas_call(
        _rglru_kernel,
        out_shape=jax.ShapeDtypeStruct((bsz, seq, LRU_W), F32),
        grid=(nct,),
        in_specs=[pl.BlockSpec((bsz, seq, tc), lambda i: (0, 0, col_x // tc + i)),
                  pl.BlockSpec((bsz, seq, tc), lambda i: (0, 0, col_gate // tc + i)),
                  pl.BlockSpec((LRU_CONV, tc), vec),
                  pl.BlockSpec((1, tc), vec),
                  pl.BlockSpec((2, 1, tc, tc), lambda i: (0, i, 0, 0)),
                  pl.BlockSpec((2, tc), vec),
                  pl.BlockSpec((2, 1, tc, tc), lambda i: (0, i, 0, 0)),
                  pl.BlockSpec((2, tc), vec),
                  pl.BlockSpec((2, tc), vec)],
        out_specs=pl.BlockSpec((bsz, seq, tc), lambda i: (0, 0, i)),
        scratch_shapes=[pltpu.VMEM((bsz, seq, tc), F32)] * 4,
        compiler_params=_cparams("parallel"),
        name="rglru",
    )(proj3, proj3, conv_w, conv_b.reshape(1, -1), wa, b_a, wx, b_x, sp)


def _dft_tables(seq):
    n = 3 * seq // 2
    kf = np.arange(n // 2, dtype=np.int64)[:, None]
    s = np.arange(seq, dtype=np.int64)[None, :]
    ang = (np.pi / n) * (((2 * kf + 1) * s) % (2 * n)).astype(np.float64)
    fc = np.cos(ang)
    fs = -np.sin(ang)
    shift = (np.pi / n) * (((2 * kf + 1) * (seq // 2)) % (2 * n)).astype(np.float64)
    to_bf16 = lambda a: jnp.asarray(a.astype(np.float32)).astype(BF16)
    return dict(n=n, fc=to_bf16(fc), fs=to_bf16(fs), fct=to_bf16(fc.T), fst=to_bf16(fs.T),
                pc=jnp.asarray(np.cos(shift).astype(np.float32)), ps=jnp.asarray(np.sin(shift).astype(np.float32)))


def _hyena_pre_kernel(x0_ref, x1_ref, v_ref, w0_ref, w1_ref, w2_ref, b0_ref, b1_ref, b2_ref,
                      vp_ref, vpb_ref, x0c_ref):
    x0c_ref[0] = _dwconv_rows(x0_ref[0], w0_ref, b0_ref)
    vp = _dwconv_rows(v_ref[0], w2_ref, b2_ref) * _dwconv_rows(x1_ref[0], w1_ref, b1_ref)
    vp_ref[0] = vp
    vpb_ref[0] = vp.astype(BF16)


def _hyena_filter_kernel(z_ref, t_ref, dl_ref, w1_ref, b1_ref, w2_ref, b2_ref, w3_ref, b3_ref, w4_ref, fr_ref,
                         fc_ref, fs_ref, pc_ref, ps_ref, hr_ref, hi_ref, filt_scr):
    @pl.when(pl.program_id(0) == 0)
    def _():
        fr = fr_ref[...]
        hid = jnp.sin(fr * (jnp.dot(z_ref[...].astype(BF16), w1_ref[...].astype(BF16),
                                    preferred_element_type=F32) + b1_ref[...]))
        hid = jnp.sin(fr * (jnp.dot(hid.astype(BF16), w2_ref[...].astype(BF16),
                                    preferred_element_type=F32) + b2_ref[...]))
        hid = jnp.sin(fr * (jnp.dot(hid.astype(BF16), w3_ref[...].astype(BF16),
                                    preferred_element_type=F32) + b3_ref[...]))
        filt = jnp.dot(hid.astype(BF16), w4_ref[...].astype(BF16), preferred_element_type=F32)
        filt = filt * jnp.exp(-2.0 * jnp.abs(t_ref[...] - 0.5) * dl_ref[...])
        filt = filt / jnp.sum(jnp.abs(filt), axis=0, keepdims=True)
        filt_scr[...] = filt.astype(BF16)

    f = filt_scr[...]
    hr0 = jnp.dot(fc_ref[...], f, preferred_element_type=F32)
    hi0 = jnp.dot(fs_ref[...], f, preferred_element_type=F32)
    pc = pc_ref[...]
    ps = ps_ref[...]
    hr_ref[...] = hr0 * pc - hi0 * ps
    hi_ref[...] = hr0 * ps + hi0 * pc


def _hyena_fwd_kernel(fc_ref, fs_ref, vp_ref, hr_ref, hi_ref, yr_ref, yi_ref):
    v = vp_ref[0]
    vr = jnp.dot(fc_ref[...], v, preferred_element_type=F32)
    vi = jnp.dot(fs_ref[...], v, preferred_element_type=F32)
    hr = hr_ref[...]
    hi = hi_ref[...]
    yr_ref[0] = (vr * hr - vi * hi).astype(BF16)
    yi_ref[0] = (vr * hi + vi * hr).astype(BF16)


def _hyena_inv_kernel(fct_ref, fst_ref, yr_ref, yi_ref, vp_ref, x0c_ref, bias_ref, o_ref, *, inv_scale):
    y = (jnp.dot(fct_ref[...], yr_ref[0], preferred_element_type=F32)
         + jnp.dot(fst_ref[...], yi_ref[0], preferred_element_type=F32)) * inv_scale
    o_ref[0] = (y + vp_ref[0] * bias_ref[...]) * x0c_ref[0]


def _hyena_mixer_pallas(proj3, col0, conv_w, conv_b, fw1, fb1, fw2, fb2, fw3, fb3, fw4, freq, fft_bias,
                        tk=512, tt=512):
    bsz, seq, _ = proj3.shape
    w = fw4.shape[1]
    tc = V7X_LANES
    nct = w // tc
    cb0 = col0 // tc
    tab = _dft_tables(seq)
    nfreq = tab['n'] // 2
    cwb = lambda off: pl.BlockSpec((HY_SHORT, tc), lambda b, c: (0, off + c))
    cbb = lambda off: pl.BlockSpec((1, tc), lambda b, c: (0, off + c))
    xb = lambda off: pl.BlockSpec((1, seq, tc), lambda b, c: (b, 0, cb0 + off + c))
    ob = pl.BlockSpec((1, seq, tc), lambda b, c: (b, 0, c))
    cb2 = conv_b.reshape(1, -1)
    vp, vpb, x0c = pl.pallas_call(
        _hyena_pre_kernel,
        out_shape=(jax.ShapeDtypeStruct((bsz, seq, w), F32), jax.ShapeDtypeStruct((bsz, seq, w), BF16),
                   jax.ShapeDtypeStruct((bsz, seq, w), F32)),
        grid=(bsz, nct),
        in_specs=[xb(0), xb(nct), xb(2 * nct), cwb(0), cwb(nct), cwb(2 * nct), cbb(0), cbb(nct), cbb(2 * nct)],
        out_specs=(ob, ob, ob),
        compiler_params=_cparams("parallel", "parallel"),
        name="hyena_pre",
    )(proj3, proj3, proj3, conv_w, conv_w, conv_w, cb2, cb2, cb2)

    tcol = np.linspace(0.0, 1.0, seq, dtype=np.float32)[:, None]
    wcol = ((2.0 * math.pi / seq) * np.arange(seq, dtype=np.float32))[:, None]
    bands = np.linspace(1e-4, HY_BANDS - 1, HY_BANDS, dtype=np.float32)[None, :]
    z = np.concatenate([tcol, np.cos(bands * wcol), -np.sin(bands * wcol)], axis=-1).astype(np.float32)
    zp = np.zeros((seq, tc), np.float32)
    zp[:, :HY_EMB] = z
    w1p = jnp.zeros((tc, HY_ORDER), F32).at[:HY_EMB].set(fw1)
    deltas = np.abs(np.linspace(math.log(HY_TARGET) / HY_SLOW, math.log(HY_TARGET) / HY_FAST, w,
                                dtype=np.float32))[None, :]
    full = lambda a: pl.BlockSpec(a.shape, lambda j: (0,) * a.ndim)
    row = lambda a: a.reshape(1, -1)
    small = [jnp.asarray(zp), jnp.asarray(tcol), jnp.asarray(deltas), w1p, row(fb1), fw2, row(fb2), fw3, row(fb3),
             fw4, row(freq)]
    hr, hi = pl.pallas_call(
        _hyena_filter_kernel,
        out_shape=(jax.ShapeDtypeStruct((nfreq, w), F32), jax.ShapeDtypeStruct((nfreq, w), F32)),
        grid=(nfreq // tk,),
        in_specs=[full(a) for a in small]
                 + [pl.BlockSpec((tk, seq), lambda j: (j, 0)), pl.BlockSpec((tk, seq), lambda j: (j, 0)),
                    pl.BlockSpec((tk, 1), lambda j: (j, 0)), pl.BlockSpec((tk, 1), lambda j: (j, 0))],
        out_specs=(pl.BlockSpec((tk, w), lambda j: (j, 0)), pl.BlockSpec((tk, w), lambda j: (j, 0))),
        scratch_shapes=[pltpu.VMEM((seq, w), BF16)],
        compiler_params=_cparams("arbitrary"),
        name="hyena_filter",
    )(*small, tab['fc'], tab['fs'], tab['pc'], tab['ps'])

    yr, yi = pl.pallas_call(
        _hyena_fwd_kernel,
        out_shape=(jax.ShapeDtypeStruct((bsz, nfreq, w), BF16), jax.ShapeDtypeStruct((bsz, nfreq, w), BF16)),
        grid=(bsz, nfreq // tk),
        in_specs=[pl.BlockSpec((tk, seq), lambda b, j: (j, 0)), pl.BlockSpec((tk, seq), lambda b, j: (j, 0)),
                  pl.BlockSpec((1, seq, w), lambda b, j: (b, 0, 0)),
                  pl.BlockSpec((tk, w), lambda b, j: (j, 0)), pl.BlockSpec((tk, w), lambda b, j: (j, 0))],
        out_specs=(pl.BlockSpec((1, tk, w), lambda b, j: (b, j, 0)), pl.BlockSpec((1, tk, w), lambda b, j: (b, j, 0))),
        compiler_params=_cparams("parallel", "parallel"),
        name="hyena_fwd",
    )(tab['fc'], tab['fs'], vpb, hr, hi)

    return pl.pallas_call(
        partial(_hyena_inv_kernel, inv_scale=2.0 / tab['n']),
        out_shape=jax.ShapeDtypeStruct((bsz, seq, w), F32),
        grid=(bsz, seq // tt),
        in_specs=[pl.BlockSpec((tt, nfreq), lambda b, j: (j, 0)), pl.BlockSpec((tt, nfreq), lambda b, j: (j, 0)),
                  pl.BlockSpec((1, nfreq, w), lambda b, j: (b, 0, 0)), pl.BlockSpec((1, nfreq, w), lambda b, j: (b, 0, 0)),
                  pl.BlockSpec((1, tt, w), lambda b, j: (b, j, 0)), pl.BlockSpec((1, tt, w), lambda b, j: (b, j, 0)),
                  pl.BlockSpec((1, w), lambda b, j: (0, 0))],
        out_specs=pl.BlockSpec((1, tt, w), lambda b, j: (b, j, 0)),
        compiler_params=_cparams("parallel", "parallel"),
        name="hyena_inv",
    )(tab['fct'], tab['fst'], yr, yi, vp, x0c, fft_bias.reshape(1, w))


def _split3_bf16(x):
    hi = x.astype(BF16)
    r1 = x - hi.astype(F32)
    mid = r1.astype(BF16)
    lo = (r1 - mid.astype(F32)).astype(BF16)
    return hi, mid, lo


def _exact_tri_matmul(tri, x):
    hi, mid, lo = _split3_bf16(x)
    return (jnp.dot(tri, hi, preferred_element_type=F32) + jnp.dot(tri, mid, preferred_element_type=F32)
            + jnp.dot(tri, lo, preferred_element_type=F32))


def _silu(x):
    return x * jax.nn.sigmoid(x)


def _ssd_kernel(z_ref, xs_ref, bm_ref, cm_ref, dt_ref, cwx_ref, cbx_ref, cwb_ref, cbb_ref, cwc_ref, cbc_ref,
                dtb_ref, alog_ref, dsk_ref, ng_ref, o_ref, xs_scr, bm_scr, cm_scr, dt_scr, da_scr, yb_scr, st_scr,
                *, chunk, head_dim, heads):
    seq = xs_ref.shape[1]
    lanes = V7X_LANES
    n_chunks = seq // chunk
    pairs = heads * head_dim // lanes
    per = lanes // head_dim

    xs_scr[...] = _silu(_dwconv_rows(xs_ref[0], cwx_ref, cbx_ref))
    bm_scr[...] = _silu(_dwconv_rows(bm_ref[0], cwb_ref, cbb_ref))
    cm_scr[...] = _silu(_dwconv_rows(cm_ref[0], cwc_ref, cbc_ref))
    raw = dt_ref[0, 0] + dtb_ref[0]
    dt = jnp.maximum(raw, 0.0) + jnp.log(1.0 + jnp.exp(-jnp.abs(raw)))
    dt_scr[...] = dt
    da_scr[...] = dt * (-jnp.exp(alog_ref[0]))
    st_scr[...] = jnp.zeros_like(st_scr)

    r_i = lax.broadcasted_iota(jnp.int32, (chunk, chunk), 0)
    c_i = lax.broadcasted_iota(jnp.int32, (chunk, chunk), 1)
    tri_lo = (c_i <= r_i).astype(BF16)
    tri_up = (c_i >= r_i).astype(BF16)
    lane = lax.broadcasted_iota(jnp.int32, (1, lanes), 1)
    head_mask = [((lane >= hh * head_dim) & (lane < (hh + 1) * head_dim)).astype(F32) for hh in range(per)]

    def by_head(cols):
        out = cols[0] * head_mask[0]
        for hh in range(1, per):
            out = out + cols[hh] * head_mask[hh]
        return out

    def one_chunk(c, reverse):
        rows = pl.ds(pl.multiple_of(c * chunk, chunk), chunk)
        da = da_scr[rows, :]
        dtc = dt_scr[rows, :]
        cum = _exact_tri_matmul(tri_up if reverse else tri_lo, da)
        cum_t = cum.T
        edge = cum[0:1, :] if reverse else cum[chunk - 1:chunk, :]
        keep = (c_i >= r_i) if reverse else (c_i <= r_i)
        bmat = bm_scr[rows, :]
        cmat = cm_scr[rows, :].astype(BF16)
        cb = lax.dot_general(cmat, bmat.astype(BF16), (((1,), (1,)), ((), ())), preferred_element_type=F32)
        bmat_t = bmat.T.astype(BF16)
        off = heads if reverse else 0
        outs = []
        for p in range(pairs):
            xs = xs_scr[rows, p * lanes:(p + 1) * lanes]
            hs = [off + p * per + hh for hh in range(per)]
            y = None
            for hh, h in enumerate(hs):
                seg = cum[:, h:h + 1] - cum_t[h:h + 1, :]
                lmat = jnp.exp(jnp.where(keep, seg, -1e30))
                xd_h = (xs * dtc[:, h:h + 1] * head_mask[hh]).astype(BF16)
                term = jnp.dot((cb * lmat).astype(BF16), xd_h, preferred_element_type=F32)
                y = term if y is None else y + term
            sidx = (pairs if reverse else 0) + p
            state = st_scr[sidx]
            y = y + (jnp.dot(cmat, state.astype(BF16), preferred_element_type=F32)
                     * by_head([jnp.exp(cum[:, h:h + 1]) for h in hs]))
            xd = xs * by_head([dtc[:, h:h + 1] for h in hs])
            decay_s = by_head([jnp.exp(edge[:, h:h + 1] - cum[:, h:h + 1]) for h in hs])
            chunk_decay = by_head([jnp.exp(edge[:, h:h + 1]) for h in hs])
            st_scr[sidx] = chunk_decay * state + jnp.dot(bmat_t, (xd * decay_s).astype(BF16),
                                                         preferred_element_type=F32)
            outs.append(y)
        return rows, jnp.concatenate(outs, axis=1)

    def body(i, carry):
        rows_f, y_f = one_chunk(i, False)
        o_ref[0, rows_f, :] = y_f
        rows_b, y_b = one_chunk(n_chunks - 1 - i, True)
        yb_scr[rows_b, :] = y_b
        return carry

    lax.fori_loop(0, n_chunks, body, 0)

    y = (o_ref[0] + yb_scr[...] + dsk_ref[...] * xs_scr[...]) * _silu(z_ref[0])
    y = y * lax.rsqrt(jnp.mean(y * y, axis=-1, keepdims=True) + RMS_EPS)
    o_ref[0] = y * ng_ref[...]


def _ssd_mixer_pallas(proj3, col_z, dt_raw, conv_w, conv_b, dt_bias, a_log, d_skip, norm_g):
    bsz, seq, _ = proj3.shape
    lanes = V7X_LANES
    g = SSD_GROUPS
    hg = SSD_HEADS // g
    gw = SSD_W // g
    col_x = col_z + SSD_W
    col_b = col_x + SSD_W
    col_c = col_b + g * SSD_STATE

    def per_group(v):
        lead = v.shape[:-2]
        v = v.reshape(lead + (2, g, hg))
        v = jnp.moveaxis(v, -2, 0).reshape((g,) + lead + (2 * hg,))
        return jnp.pad(v, [(0, 0)] * (v.ndim - 1) + [(0, lanes - 2 * hg)])
    dtg = jnp.moveaxis(per_group(dt_raw[..., :2 * SSD_HEADS].reshape(bsz, seq, 2, SSD_HEADS)), 0, 1)
    dtb = per_group(dt_bias.astype(F32)).reshape(g, 1, lanes)
    alog = per_group(a_log.astype(F32)).reshape(g, 1, lanes)
    dsk = jnp.repeat(d_skip.astype(F32), SSD_HD).reshape(1, SSD_W)
    cb2 = conv_b.reshape(1, -1)
    nsb = SSD_W // SSD_STATE
    kern = partial(_ssd_kernel, chunk=SSD_CHUNK, head_dim=SSD_HD, heads=hg)
    return pl.pallas_call(
        kern,
        out_shape=jax.ShapeDtypeStruct((bsz, seq, SSD_W), F32),
        grid=(bsz, g),
        in_specs=[pl.BlockSpec((1, seq, gw), lambda b, gi: (b, 0, col_z // gw + gi)),
                  pl.BlockSpec((1, seq, gw), lambda b, gi: (b, 0, col_x // gw + gi)),
                  pl.BlockSpec((1, seq, SSD_STATE), lambda b, gi: (b, 0, col_b // SSD_STATE + gi)),
                  pl.BlockSpec((1, seq, SSD_STATE), lambda b, gi: (b, 0, col_c // SSD_STATE + gi)),
                  pl.BlockSpec((1, 1, seq, lanes), lambda b, gi: (b, gi, 0, 0)),
                  pl.BlockSpec((SSD_CONV, gw), lambda b, gi: (0, gi)),
                  pl.BlockSpec((1, gw), lambda b, gi: (0, gi)),
                  pl.BlockSpec((SSD_CONV, SSD_STATE), lambda b, gi: (0, nsb + gi)),
                  pl.BlockSpec((1, SSD_STATE), lambda b, gi: (0, nsb + gi)),
                  pl.BlockSpec((SSD_CONV, SSD_STATE), lambda b, gi: (0, nsb + g + gi)),
                  pl.BlockSpec((1, SSD_STATE), lambda b, gi: (0, nsb + g + gi)),
                  pl.BlockSpec((1, 1, lanes), lambda b, gi: (gi, 0, 0)),
                  pl.BlockSpec((1, 1, lanes), lambda b, gi: (gi, 0, 0)),
                  pl.BlockSpec((1, gw), lambda b, gi: (0, gi)),
                  pl.BlockSpec((1, gw), lambda b, gi: (0, gi))],
        out_specs=pl.BlockSpec((1, seq, gw), lambda b, gi: (b, 0, gi)),
        scratch_shapes=[pltpu.VMEM((seq, gw), F32), pltpu.VMEM((seq, SSD_STATE), F32),
                        pltpu.VMEM((seq, SSD_STATE), F32), pltpu.VMEM((seq, lanes), F32),
                        pltpu.VMEM((seq, lanes), F32), pltpu.VMEM((seq, gw), F32),
                        pltpu.VMEM((2 * gw // lanes, SSD_STATE, lanes), F32)],
        compiler_params=_cparams("parallel", "parallel"),
        name="ssd",
    )(proj3, proj3, proj3, proj3, dtg, conv_w, cb2, conv_w, cb2, conv_w, cb2, dtb, alog, dsk,
      norm_g.reshape(1, SSD_W))


def _t5_bucket(rel):
    nb = REL_BUCKETS // 2
    ret = (rel > 0).astype(jnp.int32) * nb
    n = jnp.abs(rel)
    max_exact = nb // 2
    large = max_exact + (jnp.log(jnp.maximum(n, 1).astype(jnp.float32) / max_exact)
                         / math.log(REL_MAX_DIST / max_exact) * (nb - max_exact)).astype(jnp.int32)
    large = jnp.minimum(large, nb - 1)
    return ret + jnp.where(n < max_exact, n, large)


MIX_COLS = HY_COLS + LRU_COLS + DA_COLS + SSD_W + SSD_XBC
DT_COLS = 2 * SSD_HEADS
MIX_PAD = MIX_COLS + V7X_LANES


def kernel(x, norm1_g, w_in, hy_conv_w, hy_conv_b, hy_fw1, hy_fb1, hy_fw2, hy_fb2, hy_fw3, hy_fb3,
           hy_fw4, hy_freq, hy_bias, lru_conv_w, lru_conv_b, lru_wa, lru_ba, lru_wx, lru_bx, lru_lam,
           da_lam, da_subln_g, ssd_conv_w, ssd_conv_b, ssd_dt_bias, ssd_a_log, ssd_d, ssd_norm_g,
           w_branch, b_gate, w_out, norm2_g, w_router, moe_w1, moe_w3, moe_w2, rel_bias, final_g):
    bsz, seq_len, d_model = x.shape
    t = bsz * seq_len
    xt = x.reshape(t, d_model)
    o1 = HY_COLS
    o2 = o1 + LRU_COLS
    o3 = o2 + DA_COLS
    o4 = o3 + SSD_W + SSD_XBC
    for l in range(DEPTH):
        g1 = norm1_g[l].reshape(1, d_model)
        wq = w_in[l].astype(BF16)
        proj, dt_raw = _inproj(xt, g1, wq, MIX_COLS)
        p3 = proj.reshape(bsz, seq_len, MIX_COLS)
        o_hy = _hyena_mixer_pallas(p3, 0, hy_conv_w[l], hy_conv_b[l], hy_fw1[l], hy_fb1[l], hy_fw2[l],
                                   hy_fb2[l], hy_fw3[l], hy_fb3[l], hy_fw4[l], hy_freq[l], hy_bias[l])
        o_lru = _rglru_mixer(p3, o1, o1 + LRU_W, lru_conv_w[l], lru_conv_b[l],
                             lru_wa[l], lru_ba[l], lru_wx[l], lru_bx[l], lru_lam[l])
        lam_init = 0.8 - 0.6 * math.exp(-0.3 * l)
        o_da = _diff_attention(proj, o2, bsz, seq_len, da_lam[l], da_subln_g[l], rel_bias, lam_init)
        o_ssd = _ssd_mixer_pallas(p3, o3, dt_raw.reshape(bsz, seq_len, V7X_LANES), ssd_conv_w[l], ssd_conv_b[l],
                                  ssd_dt_bias[l], ssd_a_log[l], ssd_d[l], ssd_norm_g[l])
        branches = [o_hy.reshape(t, BR_W), o_lru.reshape(t, BR_W), o_da, o_ssd.reshape(t, BR_W)]
        xt = _gated_merge(xt, g1, wq, MIX_COLS + DT_COLS, b_gate[l], branches, w_branch[l].astype(BF16),
                          w_out[l].astype(BF16))
        xt = _expert_choice_ffn(xt, bsz, seq_len, norm2_g[l], w_router[l], moe_w1, moe_w3, moe_w2, l)
    return _final_norm(xt, final_g).reshape(bsz, seq_len, d_model)
```

```python
import math
from functools import partial

import numpy as np
import jax
import jax.numpy as jnp
from jax import lax
from jax.experimental import pallas as pl
from jax.experimental.pallas import tpu as pltpu

D_MODEL = 2048
BATCH = 4
SEQ = 2048
DEPTH = 2

N_BRANCH = 4
BR_W = D_MODEL // 4
RMS_EPS = 1e-6

HY_W = BR_W
HY_SHORT = 3
HY_EMB = 33
HY_BANDS = (HY_EMB - 1) // 2
HY_ORDER = 64
HY_TARGET = 1e-2
HY_FAST = 0.3
HY_SLOW = 1.5

LRU_W = BR_W
LRU_HEADS = 8
LRU_HD = LRU_W // LRU_HEADS
LRU_CONV = 4
LRU_C = 8.0

DA_HEADS = 4
DA_HD = BR_W // (2 * DA_HEADS)
DA_QBLOCK = 128
REL_BUCKETS = 32
REL_MAX_DIST = 128

SSD_W = BR_W
SSD_HD = 64
SSD_HEADS = SSD_W // SSD_HD
SSD_GROUPS = 2
SSD_STATE = 128
SSD_CONV = 4
SSD_CHUNK = 128

N_EXPERTS = 16
EC_CAPACITY = 2
D_EXPERT = D_MODEL

HY_COLS = 3 * HY_W
LRU_COLS = 2 * LRU_W
DA_COLS = 3 * DA_HEADS * 2 * DA_HD
SSD_XBC = SSD_W + 2 * SSD_GROUPS * SSD_STATE
SSD_COLS = SSD_W + SSD_XBC + 2 * SSD_HEADS
GATE_COLS = N_BRANCH * D_MODEL
IN_COLS = HY_COLS + LRU_COLS + DA_COLS + SSD_COLS + GATE_COLS

V7X_LANES = 128
V7X_VMEM_BYTES = 64 * 1024 * 1024
VMEM_LIMIT_BYTES = V7X_VMEM_BYTES - 8 * 1024 * 1024

BF16 = jnp.bfloat16
F32 = jnp.float32


def _cparams(*sem):
    return pltpu.CompilerParams(dimension_semantics=sem, vmem_limit_bytes=VMEM_LIMIT_BYTES)


def _rms_rows(x, g, eps):
    ms = jnp.mean(x * x, axis=-1, keepdims=True)
    return x * lax.rsqrt(ms + eps) * g


def _inproj_kernel(x_ref, g_ref, w_ref, ws_ref, o_ref, os_ref, h_scr):
    @pl.when(pl.program_id(1) == 0)
    def _():
        h_scr[...] = _rms_rows(x_ref[...], g_ref[...], RMS_EPS).astype(BF16)
        os_ref[...] = jnp.dot(h_scr[...], ws_ref[...], preferred_element_type=F32)

    o_ref[...] = jnp.dot(h_scr[...], w_ref[...], preferred_element_type=F32)


def _inproj(xt, g, wq, n, tm=1024, tn=512):
    t, d = xt.shape
    ns = V7X_LANES
    return pl.pallas_call(
        _inproj_kernel,
        out_shape=(jax.ShapeDtypeStruct((t, n), F32), jax.ShapeDtypeStruct((t, ns), F32)),
        grid=(t // tm, n // tn),
        in_specs=[pl.BlockSpec((tm, d), lambda i, j: (i, 0)),
                  pl.BlockSpec((1, d), lambda i, j: (0, 0)),
                  pl.BlockSpec((d, tn), lambda i, j: (0, j)),
                  pl.BlockSpec((d, ns), lambda i, j: (0, n // ns))],
        out_specs=(pl.BlockSpec((tm, tn), lambda i, j: (i, j)), pl.BlockSpec((tm, ns), lambda i, j: (i, 0))),
        scratch_shapes=[pltpu.VMEM((tm, d), BF16)],
        compiler_params=_cparams("parallel", "arbitrary"),
        name="inproj",
    )(xt, g, wq, wq)


def _diffattn_kernel(relb_ref, q_ref, k_ref, v_ref, bucket_ref, lamqk_ref, g_ref, o_ref, *,
                     tq, lam_init, head_dim):
    h = pl.program_id(1)
    seq = q_ref.shape[0]
    margin = REL_MAX_DIST
    bucket = bucket_ref[...]
    band = jnp.zeros(bucket.shape, F32)
    for b in range(REL_BUCKETS):
        band = jnp.where(bucket == b, relb_ref[h, b], band)
    c_neg = band[0:1, 0:1]
    c_pos = band[tq - 1:tq, tq + 2 * margin - 1:tq + 2 * margin]

    lq = lamqk_ref[...]
    lam = (jnp.exp(jnp.sum(lq[0:1] * lq[1:2], axis=-1, keepdims=True))
           - jnp.exp(jnp.sum(lq[2:3] * lq[3:4], axis=-1, keepdims=True)) + lam_init)

    lane = lax.broadcasted_iota(jnp.int32, (1, 2 * head_dim), 1)
    lo_mask = (lane < head_dim).astype(F32)
    hi_mask = 1.0 - lo_mask
    k = k_ref[...].astype(BF16)
    v_aug = jnp.concatenate([v_ref[...].astype(BF16), jnp.ones((seq, 2 * head_dim), BF16)], axis=1)
    hw = 2 * head_dim
    scale = head_dim ** -0.5
    dn = (((1,), (1,)), ((), ()))
    for i in range(seq // tq):
        q0 = i * tq
        q = q_ref[q0:q0 + tq, :] * scale
        q1 = (q * lo_mask).astype(BF16)
        q2 = (q * hi_mask).astype(BF16)
        lo = max(q0 - margin, 0)
        hi = min(q0 + tq + margin, seq)
        pieces = []
        if lo > 0:
            pieces.append(jnp.broadcast_to(c_neg, (tq, lo)))
        pieces.append(band[:, lo - (q0 - margin):hi - (q0 - margin)])
        if hi < seq:
            pieces.append(jnp.broadcast_to(c_pos, (tq, seq - hi)))
        bias = jnp.concatenate(pieces, axis=1) if len(pieces) > 1 else pieces[0]
        s1 = lax.dot_general(q1, k, dn, preferred_element_type=F32) + bias
        s2 = lax.dot_general(q2, k, dn, preferred_element_type=F32) + bias
        p1 = jnp.exp(s1 - jnp.max(s1, axis=-1, keepdims=True)).astype(BF16)
        p2 = jnp.exp(s2 - jnp.max(s2, axis=-1, keepdims=True)).astype(BF16)
        pv1 = jnp.dot(p1, v_aug, preferred_element_type=F32)
        pv2 = jnp.dot(p2, v_aug, preferred_element_type=F32)
        o = pv1[:, :hw] / pv1[:, hw:hw + 1] - pv2[:, :hw] * (lam / pv2[:, hw:hw + 1])
        o = _rms_rows(o, g_ref[...], 1e-5) * (1.0 - lam_init)
        o_ref[q0:q0 + tq, :] = o


def _diff_attention(proj, col0, bsz, seq, lam_qk, subln_g, rel_bias, lam_init, tq=256):
    hw = 2 * DA_HD
    cb = col0 // hw
    margin = REL_MAX_DIST
    r = jnp.arange(tq)[:, None]
    c = jnp.arange(tq + 2 * margin)[None, :]
    bucket = _t5_bucket(c - margin - r)
    kern = partial(_diffattn_kernel, tq=tq, lam_init=lam_init, head_dim=DA_HD)
    return pl.pallas_call(
        kern,
        out_shape=jax.ShapeDtypeStruct((bsz * seq, DA_HEADS * hw), F32),
        grid=(bsz, DA_HEADS),
        in_specs=[pl.BlockSpec(memory_space=pltpu.SMEM),
                  pl.BlockSpec((seq, hw), lambda b, h: (b, cb + h)),
                  pl.BlockSpec((seq, hw), lambda b, h: (b, cb + DA_HEADS + h)),
                  pl.BlockSpec((seq, hw), lambda b, h: (b, cb + 2 * DA_HEADS + h)),
                  pl.BlockSpec(bucket.shape, lambda b, h: (0, 0)),
                  pl.BlockSpec(lam_qk.shape, lambda b, h: (0, 0)),
                  pl.BlockSpec((1, hw), lambda b, h: (0, 0))],
        out_specs=pl.BlockSpec((seq, hw), lambda b, h: (b, h)),
        compiler_params=_cparams("parallel", "parallel"),
        name="diffattn",
    )(rel_bias.T, proj, proj, proj, bucket, lam_qk, subln_g.reshape(1, hw))


def _merge_kernel(x_ref, g_ref, wg0_ref, wg1_ref, wg2_ref, wg3_ref, bg_ref, b0_ref, b1_ref, b2_ref, b3_ref,
                  wb_ref, wo_ref, o_ref, h_scr, br_scr, acc_scr, *, shift):
    j = pl.program_id(1)
    wg_ref = (wg0_ref, wg1_ref, wg2_ref, wg3_ref)

    @pl.when(j == 0)
    def _():
        h_scr[...] = _rms_rows(x_ref[...], g_ref[...], RMS_EPS).astype(BF16)
        for kk, b_ref in enumerate((b0_ref, b1_ref, b2_ref, b3_ref)):
            br_scr[kk] = b_ref[...].astype(BF16)
        acc_scr[...] = jnp.zeros_like(acc_scr)

    h = h_scr[...]
    tc = wo_ref.shape[0]
    c_pos = j * tc + lax.broadcasted_iota(jnp.int32, (1, tc), 1)
    valid = (c_pos >= shift) & (c_pos < shift + x_ref.shape[1])
    m = None
    for kk in range(N_BRANCH):
        gate = jnp.dot(h, wg_ref[kk][...], preferred_element_type=F32) + bg_ref[kk:kk + 1, :]
        bp = jnp.dot(br_scr[kk], wb_ref[kk], preferred_element_type=F32)
        term = jax.nn.sigmoid(gate) * bp
        m = term if m is None else m + term
    m = jnp.where(valid, m, 0.0)
    acc_scr[...] += jnp.dot(m.astype(BF16), wo_ref[...], preferred_element_type=F32)

    @pl.when(j == pl.num_programs(1) - 1)
    def _():
        o_ref[...] = x_ref[...] + acc_scr[...]


def _gated_merge(xt, g, wq, gate_col0, bg, branches, wb, wo, tm=512, tc=256):
    t, d = xt.shape
    bw = branches[0].shape[1]
    shift = gate_col0 % tc
    blk0 = gate_col0 // tc
    ncb = d // tc + 1
    dp = ncb * tc
    bg_s = jnp.pad(bg, ((0, 0), (shift, dp - d - shift)))
    wb_s = jnp.pad(wb, ((0, 0), (0, 0), (shift, dp - d - shift)))
    wo_s = jnp.pad(wo, ((shift, dp - d - shift), (0, 0)))
    return pl.pallas_call(
        partial(_merge_kernel, shift=shift),
        out_shape=jax.ShapeDtypeStruct((t, d), F32),
        grid=(t // tm, ncb),
        in_specs=[pl.BlockSpec((tm, d), lambda i, j: (i, 0)),
                  pl.BlockSpec((1, d), lambda i, j: (0, 0))]
                 + [pl.BlockSpec((d, tc), lambda i, j, kk=kk: (0, blk0 + kk * (d // tc) + j))
                    for kk in range(N_BRANCH)]
                 + [pl.BlockSpec((N_BRANCH, tc), lambda i, j: (0, j))]
                 + [pl.BlockSpec((tm, bw), lambda i, j: (i, 0))] * N_BRANCH
                 + [pl.BlockSpec((N_BRANCH, bw, tc), lambda i, j: (0, 0, j)),
                    pl.BlockSpec((tc, d), lambda i, j: (j, 0))],
        out_specs=pl.BlockSpec((tm, d), lambda i, j: (i, 0)),
        scratch_shapes=[pltpu.VMEM((tm, d), BF16), pltpu.VMEM((N_BRANCH, tm, bw), BF16),
                        pltpu.VMEM((tm, d), F32)],
        compiler_params=_cparams("parallel", "arbitrary"),
        name="gated_merge",
    )(xt, g, wq, wq, wq, wq, bg_s, *branches, wb_s, wo_s)


def _pack_bf16_halves(h):
    half = h.shape[1] // 2
    bits = lax.bitcast_convert_type(h.astype(F32), jnp.uint32)
    return (bits[:, :half] >> 16) | (bits[:, half:] & jnp.uint32(0xFFFF0000))


def _unpack_bf16_halves(p):
    lo = lax.bitcast_convert_type(p << 16, F32).astype(BF16)
    hi = lax.bitcast_convert_type(p & jnp.uint32(0xFFFF0000), F32).astype(BF16)
    return lo, hi


def _router_kernel(x_ref, g_ref, wr_ref, h_ref, aff_ref, *, n_experts):
    h = _rms_rows(x_ref[...], g_ref[...], RMS_EPS).astype(BF16)
    h_ref[...] = _pack_bf16_halves(h)
    logits = jnp.dot(h, wr_ref[...], preferred_element_type=F32)
    lane = lax.broadcasted_iota(jnp.int32, logits.shape, 1)
    logits = jnp.where(lane < n_experts, logits, -jnp.inf)
    p = jnp.exp(logits - jnp.max(logits, axis=-1, keepdims=True))
    aff_ref[...] = p / jnp.sum(p, axis=-1, keepdims=True)


def _norm_router(xt, g, w_router, tm=512):
    t, d = xt.shape
    e = w_router.shape[1]
    wr = jnp.zeros((d, V7X_LANES), BF16).at[:, :e].set(w_router.astype(BF16))
    return pl.pallas_call(
        partial(_router_kernel, n_experts=e),
        out_shape=(jax.ShapeDtypeStruct((t, d // 2), jnp.uint32), jax.ShapeDtypeStruct((t, V7X_LANES), F32)),
        grid=(t // tm,),
        in_specs=[pl.BlockSpec((tm, d), lambda i: (i, 0)),
                  pl.BlockSpec((1, d), lambda i: (0, 0)),
                  pl.BlockSpec((d, V7X_LANES), lambda i: (0, 0))],
        out_specs=(pl.BlockSpec((tm, d // 2), lambda i: (i, 0)),
                   pl.BlockSpec((tm, V7X_LANES), lambda i: (i, 0))),
        compiler_params=_cparams("parallel"),
        name="norm_router",
    )(xt, g, wr)


def _expert_kernel(xg_ref, w1_ref, w3_ref, w2_ref, gate_ref, o_ref, x_scr, acc_scr):
    f = pl.program_id(1)
    half = x_scr.shape[1] // 2

    @pl.when(f == 0)
    def _():
        c = x_scr.shape[0]
        lanes = xg_ref.shape[2]
        sub = xg_ref.shape[1] // c
        for s in range(sub):
            lo, hi = _unpack_bf16_halves(xg_ref[0, pl.ds(s, c, stride=sub), :])
            x_scr[:, s * lanes:(s + 1) * lanes] = lo
            x_scr[:, half + s * lanes:half + (s + 1) * lanes] = hi
        acc_scr[...] = jnp.zeros_like(acc_scr)

    xg = x_scr[...]
    a = jnp.dot(xg, w1_ref[0, 0].astype(BF16), preferred_element_type=F32)
    b = jnp.dot(xg, w3_ref[0, 0].astype(BF16), preferred_element_type=F32)
    hid = (a * jax.nn.sigmoid(a) * b).astype(BF16)
    acc_scr[...] += jnp.dot(hid, w2_ref[0, 0].astype(BF16), preferred_element_type=F32)

    @pl.when(f == pl.num_programs(1) - 1)
    def _():
        o_ref[0] = (acc_scr[...] * gate_ref[0]).astype(BF16)


def _experts(xg, w1, w3, w2, layer, gate, tf=256):
    e, rows, lanes = xg.shape
    c = gate.shape[1]
    d = w1.shape[2]
    f = w1.shape[3]
    return pl.pallas_call(
        _expert_kernel,
        out_shape=jax.ShapeDtypeStruct((e, c, d), BF16),
        grid=(e, f // tf),
        in_specs=[pl.BlockSpec((1, rows, lanes), lambda i, j: (i, 0, 0)),
                  pl.BlockSpec((1, 1, d, tf), lambda i, j: (layer, i, 0, j)),
                  pl.BlockSpec((1, 1, d, tf), lambda i, j: (layer, i, 0, j)),
                  pl.BlockSpec((1, 1, tf, d), lambda i, j: (layer, i, j, 0)),
                  pl.BlockSpec((1, c, 1), lambda i, j: (i, 0, 0))],
        out_specs=pl.BlockSpec((1, c, d), lambda i, j: (i, 0, 0)),
        scratch_shapes=[pltpu.VMEM((c, d), BF16), pltpu.VMEM((c, d), F32)],
        compiler_params=_cparams("parallel", "arbitrary"),
        name="experts",
    )(xg, w1, w3, w2, gate)


def _prefix_count(mask_f, tri):
    rows, n = mask_f.shape
    w = tri.shape[0]
    run = jnp.zeros((rows, 1), F32)
    outs = []
    for c in range(n // w):
        blk = mask_f[:, c * w:(c + 1) * w]
        outs.append(jnp.dot(blk.astype(BF16), tri, preferred_element_type=F32) + run)
        run = run + jnp.sum(blk, axis=-1, keepdims=True)
    return jnp.concatenate(outs, axis=1), run


def _topk_kernel(aff_ref, idx_ref, gate_ref, *, cap):
    aff = aff_ref[0]
    n_exp, n_tok = aff.shape
    keys = lax.bitcast_convert_type(aff, jnp.int32)
    thr = jnp.zeros((n_exp, 1), jnp.int32)
    for bit in range(30, -1, -1):
        cand = thr | (1 << bit)
        cnt = jnp.sum((keys >= cand).astype(F32), axis=-1, keepdims=True)
        thr = jnp.where(cnt >= cap, cand, thr)
    gt = (keys > thr).astype(F32)
    eq = (keys == thr).astype(F32)
    w = V7X_LANES
    r_i = lax.broadcasted_iota(jnp.int32, (w, w), 0)
    c_i = lax.broadcasted_iota(jnp.int32, (w, w), 1)
    tri = (r_i < c_i).astype(BF16)
    need = cap - jnp.sum(gt, axis=-1, keepdims=True)
    eq_rank, _ = _prefix_count(eq, tri)
    sel = gt + eq * (eq_rank < need).astype(F32)
    pos, _ = _prefix_count(sel, tri)
    tok = lax.broadcasted_iota(jnp.int32, (1, n_tok), 1)
    a_h = aff.astype(BF16)
    rem = aff - a_h.astype(F32)
    a_m = rem.astype(BF16)
    a_l = (rem - a_m.astype(F32)).astype(BF16)
    slot = lax.broadcasted_iota(jnp.int32, (cap, n_tok), 0).astype(F32)
    pos = jnp.where(sel > 0.5, pos, -1.0)
    dn = (((1,), (1,)), ((), ()))
    for e in range(n_exp):
        hit = jnp.where(pos[e:e + 1, :] == slot, 1.0, 0.0).astype(BF16)
        src = jnp.concatenate([(tok >> 6).astype(F32), (tok & 63).astype(F32),
                               a_h[e:e + 1, :].astype(F32), a_m[e:e + 1, :].astype(F32),
                               a_l[e:e + 1, :].astype(F32), jnp.zeros((3, n_tok), F32)], axis=0).astype(BF16)
        res = lax.dot_general(src, hit, dn, preferred_element_type=F32)
        idx_ref[0, e:e + 1, :] = (res[0:1] * 64.0 + res[1:2]).astype(jnp.int32)
        gate_ref[0, e:e + 1, :] = res[2:3] + res[3:4] + res[4:5]


def _topk_select(aff_t, cap):
    bsz, n_exp, n_tok = aff_t.shape
    blk = lambda n: pl.BlockSpec((1, n_exp, n), lambda b: (b, 0, 0))
    return pl.pallas_call(
        partial(_topk_kernel, cap=cap),
        out_shape=(jax.ShapeDtypeStruct((bsz, n_exp, cap), jnp.int32),
                   jax.ShapeDtypeStruct((bsz, n_exp, cap), F32)),
        grid=(bsz,),
        in_specs=[blk(n_tok)],
        out_specs=(blk(cap), blk(cap)),
        compiler_params=_cparams("parallel"),
        name="topk_select",
    )(aff_t)


def _gather_kernel(rows_ref, h_hbm, o_ref, sem):
    e = pl.program_id(0)
    sub = h_hbm.shape[1]
    n = o_ref.shape[1] // sub

    def start(j, carry):
        dst = o_ref.at[0, pl.ds(pl.multiple_of(j * sub, sub), sub)]
        pltpu.make_async_copy(h_hbm.at[rows_ref[e, j]], dst, sem).start()
        return carry

    lax.fori_loop(0, n, start, 0, unroll=8)
    pltpu.make_async_copy(o_ref.at[0], o_ref.at[0], sem).wait()


def _moe_gather(h, rows):
    n_exp, c = rows.shape
    _, sub, lanes = h.shape
    return pl.pallas_call(
        _gather_kernel,
        out_shape=jax.ShapeDtypeStruct((n_exp, c * sub, lanes), h.dtype),
        grid_spec=pltpu.PrefetchScalarGridSpec(
            num_scalar_prefetch=1,
            grid=(n_exp,),
            in_specs=[pl.BlockSpec(memory_space=pl.ANY)],
            out_specs=pl.BlockSpec((1, c * sub, lanes), lambda e, rows: (e, 0, 0)),
            scratch_shapes=[pltpu.SemaphoreType.DMA(())]),
        compiler_params=_cparams("arbitrary"),
        name="moe_gather",
    )(rows, h)


def _combine_kernel(idx_ref, x_ref, y_ref, o_ref):
    tq = x_ref.shape[0]
    q0 = pl.program_id(2) * tq
    n_exp, cap, td = y_ref.shape
    tokens = q0 + lax.broadcasted_iota(jnp.int32, (tq, 1), 0)
    hit = jnp.where(idx_ref[0] == tokens, 1.0, 0.0).astype(BF16)
    y = y_ref[...].reshape(n_exp * cap, td)
    o_ref[...] = x_ref[...] + jnp.dot(hit, y, preferred_element_type=F32)


def _moe_combine(xt, y, idx, bsz, tq=512, td=512):
    t, d = xt.shape
    n_tok = t // bsz
    n_exp, _, cap = idx.shape[1], None, idx.shape[2]
    idx_flat = idx.reshape(bsz, 1, n_exp * cap)
    tq = min(tq, n_tok)
    td = min(td, d)
    nq = n_tok // tq
    return pl.pallas_call(
        _combine_kernel,
        out_shape=jax.ShapeDtypeStruct((t, d), F32),
        grid=(bsz, d // td, nq),
        in_specs=[pl.BlockSpec((1, 1, n_exp * cap), lambda b, j, q: (b, 0, 0)),
                  pl.BlockSpec((tq, td), lambda b, j, q: (b * nq + q, j)),
                  pl.BlockSpec((n_exp, cap, td), lambda b, j, q: (0, b, j))],
        out_specs=pl.BlockSpec((tq, td), lambda b, j, q: (b * nq + q, j)),
        compiler_params=_cparams("parallel", "parallel", "arbitrary"),
        name="moe_combine",
    )(idx_flat, xt, y)


def _expert_choice_ffn(xt, bsz, n_tok, norm_g, w_router, w1, w3, w2, layer):
    d = xt.shape[1]
    n_exp = w_router.shape[1]
    cap = EC_CAPACITY * n_tok // n_exp
    h2, aff = _norm_router(xt, norm_g.reshape(1, d), w_router)
    aff_t = jnp.swapaxes(aff[:, :n_exp].reshape(bsz, n_tok, n_exp), 1, 2)
    idx, gate = _topk_select(aff_t, cap)
    rows = idx + (jnp.arange(bsz, dtype=jnp.int32) * n_tok)[:, None, None]
    rows = jnp.swapaxes(rows, 0, 1).reshape(n_exp, bsz * cap)
    gate = jnp.swapaxes(gate, 0, 1).reshape(n_exp, bsz * cap, 1)
    xg = _moe_gather(h2.reshape(h2.shape[0], -1, V7X_LANES), rows)
    y = _experts(xg, w1, w3, w2, layer, gate)
    return _moe_combine(xt, y, idx, bsz)


def _final_norm_kernel(x_ref, g_ref, o_ref):
    o_ref[...] = _rms_rows(x_ref[...], g_ref[...], RMS_EPS)


def _final_norm(xt, g, tm=512):
    t, d = xt.shape
    return pl.pallas_call(
        _final_norm_kernel,
        out_shape=jax.ShapeDtypeStruct(xt.shape, xt.dtype),
        grid=(t // tm,),
        in_specs=[pl.BlockSpec((tm, d), lambda i: (i, 0)), pl.BlockSpec((1, d), lambda i: (0, 0))],
        out_specs=pl.BlockSpec((tm, d), lambda i: (i, 0)),
        compiler_params=_cparams("parallel"),
        name="final_norm",
    )(xt, g.reshape(1, d))


def _shift_rows(x, offset):
    n = x.shape[0]
    if offset == 0:
        return x
    row = lax.broadcasted_iota(jnp.int32, x.shape, 0)
    rolled = pltpu.roll(x, (-offset) % n, axis=0)
    valid = (row + offset >= 0) & (row + offset < n)
    return jnp.where(valid, rolled, 0.0)


def _dwconv_rows(x, w_ref, b_ref):
    taps = w_ref.shape[0]
    y = b_ref[...] + jnp.zeros_like(x)
    for kk in range(taps):
        y = y + w_ref[kk:kk + 1, :] * _shift_rows(x, kk - taps // 2)
    return y


def _gelu_tanh(x):
    return 0.5 * x * (1.0 + jnp.tanh(math.sqrt(2.0 / math.pi) * (x + 0.044715 * (x * x * x))))


def _rglru_kernel(x_ref, gate_ref, cw_ref, cb_ref, wa_ref, ba_ref, wx_ref, bx_ref, sp_ref, o_ref,
                  af_scr, bf_scr, ab_scr, bb_scr):
    nb, seq, _ = x_ref.shape
    row = lax.broadcasted_iota(jnp.int32, (seq, x_ref.shape[2]), 0)
    for b in range(nb):
        xc = _dwconv_rows(x_ref[b], cw_ref, cb_ref)
        xcb = xc.astype(BF16)
        for dr, (a_scr, b_scr) in enumerate(((af_scr, bf_scr), (ab_scr, bb_scr))):
            r = jax.nn.sigmoid(jnp.dot(xcb, wa_ref[dr, 0], preferred_element_type=F32) + ba_ref[dr:dr + 1, :])
            i = jax.nn.sigmoid(jnp.dot(xcb, wx_ref[dr, 0], preferred_element_type=F32) + bx_ref[dr:dr + 1, :])
            log_a = -LRU_C * r * sp_ref[dr:dr + 1, :]
            mult = jnp.sqrt(1.0 - jnp.exp(2.0 * log_a))
            mult = jnp.where(row == (seq - 1 if dr else 0), 1.0, mult)
            a_scr[b] = jnp.exp(log_a)
            b_scr[b] = mult * i * xc

    def step(tt, carry):
        new = []
        tb = seq - 1 - tt
        for b in range(nb):
            hf, hb = carry[b]
            hf = af_scr[b, pl.ds(tt, 1), :] * hf + bf_scr[b, pl.ds(tt, 1), :]
            hb = ab_scr[b, pl.ds(tb, 1), :] * hb + bb_scr[b, pl.ds(tb, 1), :]
            bf_scr[b, pl.ds(tt, 1), :] = hf
            bb_scr[b, pl.ds(tb, 1), :] = hb
            new.append((hf, hb))
        return tuple(new)

    zero = jnp.zeros((1, x_ref.shape[2]), F32)
    lax.fori_loop(0, seq, step, tuple((zero, zero) for _ in range(nb)), unroll=8)
    for b in range(nb):
        o_ref[b] = (bf_scr[b] + bb_scr[b]) * _gelu_tanh(gate_ref[b])


def _block_diag_tiles(w, tile):
    nd, nh, hd, _ = w.shape
    per = tile // hd
    w = w.reshape(nd, nh // per, per, hd, hd)
    eye = jnp.eye(per, dtype=w.dtype)
    bd = jnp.einsum('dgpij,pq->dgpiqj', w, eye).reshape(nd, nh // per, tile, tile)
    return bd.astype(BF16)


def _rglru_mixer(proj3, col_x, col_gate, conv_w, conv_b, w_a, b_a, w_x, b_x, lam):
    bsz, seq, _ = proj3.shape
    tc = V7X_LANES
    nct = LRU_W // tc
    sp = jax.nn.softplus(-lam.astype(F32))
    wa = _block_diag_tiles(w_a, tc)
    wx = _block_diag_tiles(w_x, tc)
    vec = lambda i: (0, i)
    return pl.pallas_call(
        _rglru_kernel,
        out_shape=jax.ShapeDtypeStruct((bsz, seq, LRU_W), F32),
        grid=(nct,),
        in_specs=[pl.BlockSpec((bsz, seq, tc), lambda i: (0, 0, col_x // tc + i)),
                  pl.BlockSpec((bsz, seq, tc), lambda i: (0, 0, col_gate // tc + i)),
                  pl.BlockSpec((LRU_CONV, tc), vec),
                  pl.BlockSpec((1, tc), vec),
                  pl.BlockSpec((2, 1, tc, tc), lambda i: (0, i, 0, 0)),
                  pl.BlockSpec((2, tc), vec),
                  pl.BlockSpec((2, 1, tc, tc), lambda i: (0, i, 0, 0)),
                  pl.BlockSpec((2, tc), vec),
                  pl.BlockSpec((2, tc), vec)],
        out_specs=pl.BlockSpec((bsz, seq, tc), lambda i: (0, 0, i)),
        scratch_shapes=[pltpu.VMEM((bsz, seq, tc), F32)] * 4,
        compiler_params=_cparams("parallel"),
        name="rglru",
    )(proj3, proj3, conv_w, conv_b.reshape(1, -1), wa, b_a, wx, b_x, sp)


def _dft_tables(seq):
    n = 3 * seq // 2
    kf = np.arange(n // 2, dtype=np.int64)[:, None]
    s = np.arange(seq, dtype=np.int64)[None, :]
    ang = (np.pi / n) * (((2 * kf + 1) * s) % (2 * n)).astype(np.float64)
    fc = np.cos(ang)
    fs = -np.sin(ang)
    shift = (np.pi / n) * (((2 * kf + 1) * (seq // 2)) % (2 * n)).astype(np.float64)
    to_bf16 = lambda a: jnp.asarray(a.astype(np.float32)).astype(BF16)
    return dict(n=n, fc=to_bf16(fc), fs=to_bf16(fs), fct=to_bf16(fc.T), fst=to_bf16(fs.T),
                pc=jnp.asarray(np.cos(shift).astype(np.float32)), ps=jnp.asarray(np.sin(shift).astype(np.float32)))


def _hyena_pre_kernel(x0_ref, x1_ref, v_ref, w0_ref, w1_ref, w2_ref, b0_ref, b1_ref, b2_ref,
                      vp_ref, vpb_ref, x0c_ref):
    x0c_ref[0] = _dwconv_rows(x0_ref[0], w0_ref, b0_ref)
    vp = _dwconv_rows(v_ref[0], w2_ref, b2_ref) * _dwconv_rows(x1_ref[0], w1_ref, b1_ref)
    vp_ref[0] = vp
    vpb_ref[0] = vp.astype(BF16)


def _hyena_filter_kernel(z_ref, t_ref, dl_ref, w1_ref, b1_ref, w2_ref, b2_ref, w3_ref, b3_ref, w4_ref, fr_ref,
                         fc_ref, fs_ref, pc_ref, ps_ref, hr_ref, hi_ref, filt_scr):
    @pl.when(pl.program_id(0) == 0)
    def _():
        fr = fr_ref[...]
        hid = jnp.sin(fr * (jnp.dot(z_ref[...].astype(BF16), w1_ref[...].astype(BF16),
                                    preferred_element_type=F32) + b1_ref[...]))
        hid = jnp.sin(fr * (jnp.dot(hid.astype(BF16), w2_ref[...].astype(BF16),
                                    preferred_element_type=F32) + b2_ref[...]))
        hid = jnp.sin(fr * (jnp.dot(hid.astype(BF16), w3_ref[...].astype(BF16),
                                    preferred_element_type=F32) + b3_ref[...]))
        filt = jnp.dot(hid.astype(BF16), w4_ref[...].astype(BF16), preferred_element_type=F32)
        filt = filt * jnp.exp(-2.0 * jnp.abs(t_ref[...] - 0.5) * dl_ref[...])
        filt = filt / jnp.sum(jnp.abs(filt), axis=0, keepdims=True)
        filt_scr[...] = filt.astype(BF16)

    f = filt_scr[...]
    hr0 = jnp.dot(fc_ref[...], f, preferred_element_type=F32)
    hi0 = jnp.dot(fs_ref[...], f, preferred_element_type=F32)
    pc = pc_ref[...]
    ps = ps_ref[...]
    hr_ref[...] = hr0 * pc - hi0 * ps
    hi_ref[...] = hr0 * ps + hi0 * pc


def _hyena_fwd_kernel(fc_ref, fs_ref, vp_ref, hr_ref, hi_ref, yr_ref, yi_ref):
    v = vp_ref[0]
    vr = jnp.dot(fc_ref[...], v, preferred_element_type=F32)
    vi = jnp.dot(fs_ref[...], v, preferred_element_type=F32)
    hr = hr_ref[...]
    hi = hi_ref[...]
    yr_ref[0] = (vr * hr - vi * hi).astype(BF16)
    yi_ref[0] = (vr * hi + vi * hr).astype(BF16)


def _hyena_inv_kernel(fct_ref, fst_ref, yr_ref, yi_ref, vp_ref, x0c_ref, bias_ref, o_ref, *, inv_scale):
    y = (jnp.dot(fct_ref[...], yr_ref[0], preferred_element_type=F32)
         + jnp.dot(fst_ref[...], yi_ref[0], preferred_element_type=F32)) * inv_scale
    o_ref[0] = (y + vp_ref[0] * bias_ref[...]) * x0c_ref[0]


def _hyena_mixer_pallas(proj3, col0, conv_w, conv_b, fw1, fb1, fw2, fb2, fw3, fb3, fw4, freq, fft_bias,
                        tk=512, tt=512):
    bsz, seq, _ = proj3.shape
    w = fw4.shape[1]
    tc = V7X_LANES
    nct = w // tc
    cb0 = col0 // tc
    tab = _dft_tables(seq)
    nfreq = tab['n'] // 2
    cwb = lambda off: pl.BlockSpec((HY_SHORT, tc), lambda b, c: (0, off + c))
    cbb = lambda off: pl.BlockSpec((1, tc), lambda b, c: (0, off + c))
    xb = lambda off: pl.BlockSpec((1, seq, tc), lambda b, c: (b, 0, cb0 + off + c))
    ob = pl.BlockSpec((1, seq, tc), lambda b, c: (b, 0, c))
    cb2 = conv_b.reshape(1, -1)
    vp, vpb, x0c = pl.pallas_call(
        _hyena_pre_kernel,
        out_shape=(jax.ShapeDtypeStruct((bsz, seq, w), F32), jax.ShapeDtypeStruct((bsz, seq, w), BF16),
                   jax.ShapeDtypeStruct((bsz, seq, w), F32)),
        grid=(bsz, nct),
        in_specs=[xb(0), xb(nct), xb(2 * nct), cwb(0), cwb(nct), cwb(2 * nct), cbb(0), cbb(nct), cbb(2 * nct)],
        out_specs=(ob, ob, ob),
        compiler_params=_cparams("parallel", "parallel"),
        name="hyena_pre",
    )(proj3, proj3, proj3, conv_w, conv_w, conv_w, cb2, cb2, cb2)

    tcol = np.linspace(0.0, 1.0, seq, dtype=np.float32)[:, None]
    wcol = ((2.0 * math.pi / seq) * np.arange(seq, dtype=np.float32))[:, None]
    bands = np.linspace(1e-4, HY_BANDS - 1, HY_BANDS, dtype=np.float32)[None, :]
    z = np.concatenate([tcol, np.cos(bands * wcol), -np.sin(bands * wcol)], axis=-1).astype(np.float32)
    zp = np.zeros((seq, tc), np.float32)
    zp[:, :HY_EMB] = z
    w1p = jnp.zeros((tc, HY_ORDER), F32).at[:HY_EMB].set(fw1)
    deltas = np.abs(np.linspace(math.log(HY_TARGET) / HY_SLOW, math.log(HY_TARGET) / HY_FAST, w,
                                dtype=np.float32))[None, :]
    full = lambda a: pl.BlockSpec(a.shape, lambda j: (0,) * a.ndim)
    row = lambda a: a.reshape(1, -1)
    small = [jnp.asarray(zp), jnp.asarray(tcol), jnp.asarray(deltas), w1p, row(fb1), fw2, row(fb2), fw3, row(fb3),
             fw4, row(freq)]
    hr, hi = pl.pallas_call(
        _hyena_filter_kernel,
        out_shape=(jax.ShapeDtypeStruct((nfreq, w), F32), jax.ShapeDtypeStruct((nfreq, w), F32)),
        grid=(nfreq // tk,),
        in_specs=[full(a) for a in small]
                 + [pl.BlockSpec((tk, seq), lambda j: (j, 0)), pl.BlockSpec((tk, seq), lambda j: (j, 0)),
                    pl.BlockSpec((tk, 1), lambda j: (j, 0)), pl.BlockSpec((tk, 1), lambda j: (j, 0))],
        out_specs=(pl.BlockSpec((tk, w), lambda j: (j, 0)), pl.BlockSpec((tk, w), lambda j: (j, 0))),
        scratch_shapes=[pltpu.VMEM((seq, w), BF16)],
        compiler_params=_cparams("arbitrary"),
        name="hyena_filter",
    )(*small, tab['fc'], tab['fs'], tab['pc'], tab['ps'])

    yr, yi = pl.pallas_call(
        _hyena_fwd_kernel,
        out_shape=(jax.ShapeDtypeStruct((bsz, nfreq, w), BF16), jax.ShapeDtypeStruct((bsz, nfreq, w), BF16)),
        grid=(bsz, nfreq // tk),
        in_specs=[pl.BlockSpec((tk, seq), lambda b, j: (j, 0)), pl.BlockSpec((tk, seq), lambda b, j: (j, 0)),
                  pl.BlockSpec((1, seq, w), lambda b, j: (b, 0, 0)),
                  pl.BlockSpec((tk, w), lambda b, j: (j, 0)), pl.BlockSpec((tk, w), lambda b, j: (j, 0))],
        out_specs=(pl.BlockSpec((1, tk, w), lambda b, j: (b, j, 0)), pl.BlockSpec((1, tk, w), lambda b, j: (b, j, 0))),
        compiler_params=_cparams("parallel", "parallel"),
        name="hyena_fwd",
    )(tab['fc'], tab['fs'], vpb, hr, hi)

    return pl.pallas_call(
        partial(_hyena_inv_kernel, inv_scale=2.0 / tab['n']),
        out_shape=jax.ShapeDtypeStruct((bsz, seq, w), F32),
        grid=(bsz, seq // tt),
        in_specs=[pl.BlockSpec((tt, nfreq), lambda b, j: (j, 0)), pl.BlockSpec((tt, nfreq), lambda b, j: (j, 0)),
                  pl.BlockSpec((1, nfreq, w), lambda b, j: (b, 0, 0)), pl.BlockSpec((1, nfreq, w), lambda b, j: (b, 0, 0)),
                  pl.BlockSpec((1, tt, w), lambda b, j: (b, j, 0)), pl.BlockSpec((1, tt, w), lambda b, j: (b, j, 0)),
                  pl.BlockSpec((1, w), lambda b, j: (0, 0))],
        out_specs=pl.BlockSpec((1, tt, w), lambda b, j: (b, j, 0)),
        compiler_params=_cparams("parallel", "parallel"),
        name="hyena_inv",
    )(tab['fct'], tab['fst'], yr, yi, vp, x0c, fft_bias.reshape(1, w))


def _split3_bf16(x):
    hi = x.astype(BF16)
    r1 = x - hi.astype(F32)
    mid = r1.astype(BF16)
    lo = (r1 - mid.astype(F32)).astype(BF16)
    return hi, mid, lo


def _exact_tri_matmul(tri, x):
    hi, mid, lo = _split3_bf16(x)
    return (jnp.dot(tri, hi, preferred_element_type=F32) + jnp.dot(tri, mid, preferred_element_type=F32)
            + jnp.dot(tri, lo, preferred_element_type=F32))


def _silu(x):
    return x * jax.nn.sigmoid(x)


def _ssd_kernel(z_ref, xs_ref, bm_ref, cm_ref, dt_ref, cwx_ref, cbx_ref, cwb_ref, cbb_ref, cwc_ref, cbc_ref,
                dtb_ref, alog_ref, dsk_ref, ng_ref, o_ref, xs_scr, bm_scr, cm_scr, dt_scr, da_scr, yb_scr, st_scr,
                *, chunk, head_dim, heads):
    seq = xs_ref.shape[1]
    lanes = V7X_LANES
    n_chunks = seq // chunk
    pairs = heads * head_dim // lanes
    per = lanes // head_dim

    xs_scr[...] = _silu(_dwconv_rows(xs_ref[0], cwx_ref, cbx_ref))
    bm_scr[...] = _silu(_dwconv_rows(bm_ref[0], cwb_ref, cbb_ref))
    cm_scr[...] = _silu(_dwconv_rows(cm_ref[0], cwc_ref, cbc_ref))
    raw = dt_ref[0, 0] + dtb_ref[0]
    dt = jnp.maximum(raw, 0.0) + jnp.log(1.0 + jnp.exp(-jnp.abs(raw)))
    dt_scr[...] = dt
    da_scr[...] = dt * (-jnp.exp(alog_ref[0]))
    st_scr[...] = jnp.zeros_like(st_scr)

    r_i = lax.broadcasted_iota(jnp.int32, (chunk, chunk), 0)
    c_i = lax.broadcasted_iota(jnp.int32, (chunk, chunk), 1)
    tri_lo = (c_i <= r_i).astype(BF16)
    tri_up = (c_i >= r_i).astype(BF16)
    lane = lax.broadcasted_iota(jnp.int32, (1, lanes), 1)
    head_mask = [((lane >= hh * head_dim) & (lane < (hh + 1) * head_dim)).astype(F32) for hh in range(per)]

    def by_head(cols):
        out = cols[0] * head_mask[0]
        for hh in range(1, per):
            out = out + cols[hh] * head_mask[hh]
        return out

    def one_chunk(c, reverse):
        rows = pl.ds(pl.multiple_of(c * chunk, chunk), chunk)
        da = da_scr[rows, :]
        dtc = dt_scr[rows, :]
        cum = _exact_tri_matmul(tri_up if reverse else tri_lo, da)
        cum_t = cum.T
        edge = cum[0:1, :] if reverse else cum[chunk - 1:chunk, :]
        keep = (c_i >= r_i) if reverse else (c_i <= r_i)
        bmat = bm_scr[rows, :]
        cmat = cm_scr[rows, :].astype(BF16)
        cb = lax.dot_general(cmat, bmat.astype(BF16), (((1,), (1,)), ((), ())), preferred_element_type=F32)
        bmat_t = bmat.T.astype(BF16)
        off = heads if reverse else 0
        outs = []
        for p in range(pairs):
            xs = xs_scr[rows, p * lanes:(p + 1) * lanes]
            hs = [off + p * per + hh for hh in range(per)]
            y = None
            for hh, h in enumerate(hs):
                seg = cum[:, h:h + 1] - cum_t[h:h + 1, :]
                lmat = jnp.exp(jnp.where(keep, seg, -1e30))
                xd_h = (xs * dtc[:, h:h + 1] * head_mask[hh]).astype(BF16)
                term = jnp.dot((cb * lmat).astype(BF16), xd_h, preferred_element_type=F32)
                y = term if y is None else y + term
            sidx = (pairs if reverse else 0) + p
            state = st_scr[sidx]
            y = y + (jnp.dot(cmat, state.astype(BF16), preferred_element_type=F32)
                     * by_head([jnp.exp(cum[:, h:h + 1]) for h in hs]))
            xd = xs * by_head([dtc[:, h:h + 1] for h in hs])
            decay_s = by_head([jnp.exp(edge[:, h:h + 1] - cum[:, h:h + 1]) for h in hs])
            chunk_decay = by_head([jnp.exp(edge[:, h:h + 1]) for h in hs])
            st_scr[sidx] = chunk_decay * state + jnp.dot(bmat_t, (xd * decay_s).astype(BF16),
                                                         preferred_element_type=F32)
            outs.append(y)
        return rows, jnp.concatenate(outs, axis=1)

    def body(i, carry):
        rows_f, y_f = one_chunk(i, False)
        o_ref[0, rows_f, :] = y_f
        rows_b, y_b = one_chunk(n_chunks - 1 - i, True)
        yb_scr[rows_b, :] = y_b
        return carry

    lax.fori_loop(0, n_chunks, body, 0)

    y = (o_ref[0] + yb_scr[...] + dsk_ref[...] * xs_scr[...]) * _silu(z_ref[0])
    y = y * lax.rsqrt(jnp.mean(y * y, axis=-1, keepdims=True) + RMS_EPS)
    o_ref[0] = y * ng_ref[...]


def _ssd_mixer_pallas(proj3, col_z, dt_raw, conv_w, conv_b, dt_bias, a_log, d_skip, norm_g):
    bsz, seq, _ = proj3.shape
    lanes = V7X_LANES
    g = SSD_GROUPS
    hg = SSD_HEADS // g
    gw = SSD_W // g
    col_x = col_z + SSD_W
    col_b = col_x + SSD_W
    col_c = col_b + g * SSD_STATE

    def per_group(v):
        lead = v.shape[:-2]
        v = v.reshape(lead + (2, g, hg))
        v = jnp.moveaxis(v, -2, 0).reshape((g,) + lead + (2 * hg,))
        return jnp.pad(v, [(0, 0)] * (v.ndim - 1) + [(0, lanes - 2 * hg)])
    dtg = jnp.moveaxis(per_group(dt_raw[..., :2 * SSD_HEADS].reshape(bsz, seq, 2, SSD_HEADS)), 0, 1)
    dtb = per_group(dt_bias.astype(F32)).reshape(g, 1, lanes)
    alog = per_group(a_log.astype(F32)).reshape(g, 1, lanes)
    dsk = jnp.repeat(d_skip.astype(F32), SSD_HD).reshape(1, SSD_W)
    cb2 = conv_b.reshape(1, -1)
    nsb = SSD_W // SSD_STATE
    kern = partial(_ssd_kernel, chunk=SSD_CHUNK, head_dim=SSD_HD, heads=hg)
    return pl.pallas_call(
        kern,
        out_shape=jax.ShapeDtypeStruct((bsz, seq, SSD_W), F32),
        grid=(bsz, g),
        in_specs=[pl.BlockSpec((1, seq, gw), lambda b, gi: (b, 0, col_z // gw + gi)),
                  pl.BlockSpec((1, seq, gw), lambda b, gi: (b, 0, col_x // gw + gi)),
                  pl.BlockSpec((1, seq, SSD_STATE), lambda b, gi: (b, 0, col_b // SSD_STATE + gi)),
                  pl.BlockSpec((1, seq, SSD_STATE), lambda b, gi: (b, 0, col_c // SSD_STATE + gi)),
                  pl.BlockSpec((1, 1, seq, lanes), lambda b, gi: (b, gi, 0, 0)),
                  pl.BlockSpec((SSD_CONV, gw), lambda b, gi: (0, gi)),
                  pl.BlockSpec((1, gw), lambda b, gi: (0, gi)),
                  pl.BlockSpec((SSD_CONV, SSD_STATE), lambda b, gi: (0, nsb + gi)),
                  pl.BlockSpec((1, SSD_STATE), lambda b, gi: (0, nsb + gi)),
                  pl.BlockSpec((SSD_CONV, SSD_STATE), lambda b, gi: (0, nsb + g + gi)),
                  pl.BlockSpec((1, SSD_STATE), lambda b, gi: (0, nsb + g + gi)),
                  pl.BlockSpec((1, 1, lanes), lambda b, gi: (gi, 0, 0)),
                  pl.BlockSpec((1, 1, lanes), lambda b, gi: (gi, 0, 0)),
                  pl.BlockSpec((1, gw), lambda b, gi: (0, gi)),
                  pl.BlockSpec((1, gw), lambda b, gi: (0, gi))],
        out_specs=pl.BlockSpec((1, seq, gw), lambda b, gi: (b, 0, gi)),
        scratch_shapes=[pltpu.VMEM((seq, gw), F32), pltpu.VMEM((seq, SSD_STATE), F32),
                        pltpu.VMEM((seq, SSD_STATE), F32), pltpu.VMEM((seq, lanes), F32),
                        pltpu.VMEM((seq, lanes), F32), pltpu.VMEM((seq, gw), F32),
                        pltpu.VMEM((2 * gw // lanes, SSD_STATE, lanes), F32)],
        compiler_params=_cparams("parallel", "parallel"),
        name="ssd",
    )(proj3, proj3, proj3, proj3, dtg, conv_w, cb2, conv_w, cb2, conv_w, cb2, dtb, alog, dsk,
      norm_g.reshape(1, SSD_W))


def _t5_bucket(rel):
    nb = REL_BUCKETS // 2
    ret = (rel > 0).astype(jnp.int32) * nb
    n = jnp.abs(rel)
    max_exact = nb // 2
    large = max_exact + (jnp.log(jnp.maximum(n, 1).astype(jnp.float32) / max_exact)
                         / math.log(REL_MAX_DIST / max_exact) * (nb - max_exact)).astype(jnp.int32)
    large = jnp.minimum(large, nb - 1)
    return ret + jnp.where(n < max_exact, n, large)


MIX_COLS = HY_COLS + LRU_COLS + DA_COLS + SSD_W + SSD_XBC
DT_COLS = 2 * SSD_HEADS
MIX_PAD = MIX_COLS + V7X_LANES


def kernel(x, norm1_g, w_in, hy_conv_w, hy_conv_b, hy_fw1, hy_fb1, hy_fw2, hy_fb2, hy_fw3, hy_fb3,
           hy_fw4, hy_freq, hy_bias, lru_conv_w, lru_conv_b, lru_wa, lru_ba, lru_wx, lru_bx, lru_lam,
           da_lam, da_subln_g, ssd_conv_w, ssd_conv_b, ssd_dt_bias, ssd_a_log, ssd_d, ssd_norm_g,
           w_branch, b_gate, w_out, norm2_g, w_router, moe_w1, moe_w3, moe_w2, rel_bias, final_g):
    bsz, seq_len, d_model = x.shape
    t = bsz * seq_len
    xt = x.reshape(t, d_model)
    o1 = HY_COLS
    o2 = o1 + LRU_COLS
    o3 = o2 + DA_COLS
    o4 = o3 + SSD_W + SSD_XBC
    for l in range(DEPTH):
        g1 = norm1_g[l].reshape(1, d_model)
        wq = w_in[l].astype(BF16)
        proj, dt_raw = _inproj(xt, g1, wq, MIX_COLS)
        p3 = proj.reshape(bsz, seq_len, MIX_COLS)
        o_hy = _hyena_mixer_pallas(p3, 0, hy_conv_w[l], hy_conv_b[l], hy_fw1[l], hy_fb1[l], hy_fw2[l],
                                   hy_fb2[l], hy_fw3[l], hy_fb3[l], hy_fw4[l], hy_freq[l], hy_bias[l])
        o_lru = _rglru_mixer(p3, o1, o1 + LRU_W, lru_conv_w[l], lru_conv_b[l],
                             lru_wa[l], lru_ba[l], lru_wx[l], lru_bx[l], lru_lam[l])
        lam_init = 0.8 - 0.6 * math.exp(-0.3 * l)
        o_da = _diff_attention(proj, o2, bsz, seq_len, da_lam[l], da_subln_g[l], rel_bias, lam_init)
        o_ssd = _ssd_mixer_pallas(p3, o3, dt_raw.reshape(bsz, seq_len, V7X_LANES), ssd_conv_w[l], ssd_conv_b[l],
                                  ssd_dt_bias[l], ssd_a_log[l], ssd_d[l], ssd_norm_g[l])
        branches = [o_hy.reshape(t, BR_W), o_lru.reshape(t, BR_W), o_da, o_ssd.reshape(t, BR_W)]
        xt = _gated_merge(xt, g1, wq, MIX_COLS + DT_COLS, b_gate[l], branches, w_branch[l].astype(BF16),
                          w_out[l].astype(BF16))
        xt = _expert_choice_ffn(xt, bsz, seq_len, norm2_g[l], w_router[l], moe_w1, moe_w3, moe_w2, l)
    return _final_norm(xt, final_g).reshape(bsz, seq_len, d_model)
```

```python
import math
from functools import partial

import numpy as np
import jax
import jax.numpy as jnp
from jax import lax
from jax.experimental import pallas as pl
from jax.experimental.pallas import tpu as pltpu

D_MODEL = 2048
BATCH = 4
SEQ = 2048
DEPTH = 2

N_BRANCH = 4
BR_W = D_MODEL // 4
RMS_EPS = 1e-6

HY_W = BR_W
HY_SHORT = 3
HY_EMB = 33
HY_BANDS = (HY_EMB - 1) // 2
HY_ORDER = 64
HY_TARGET = 1e-2
HY_FAST = 0.3
HY_SLOW = 1.5

LRU_W = BR_W
LRU_HEADS = 8
LRU_HD = LRU_W // LRU_HEADS
LRU_CONV = 4
LRU_C = 8.0

DA_HEADS = 4
DA_HD = BR_W // (2 * DA_HEADS)
DA_QBLOCK = 128
REL_BUCKETS = 32
REL_MAX_DIST = 128

SSD_W = BR_W
SSD_HD = 64
SSD_HEADS = SSD_W // SSD_HD
SSD_GROUPS = 2
SSD_STATE = 128
SSD_CONV = 4
SSD_CHUNK = 128

N_EXPERTS = 16
EC_CAPACITY = 2
D_EXPERT = D_MODEL

HY_COLS = 3 * HY_W
LRU_COLS = 2 * LRU_W
DA_COLS = 3 * DA_HEADS * 2 * DA_HD
SSD_XBC = SSD_W + 2 * SSD_GROUPS * SSD_STATE
SSD_COLS = SSD_W + SSD_XBC + 2 * SSD_HEADS
GATE_COLS = N_BRANCH * D_MODEL
IN_COLS = HY_COLS + LRU_COLS + DA_COLS + SSD_COLS + GATE_COLS

V7X_LANES = 128
V7X_VMEM_BYTES = 64 * 1024 * 1024
VMEM_LIMIT_BYTES = V7X_VMEM_BYTES - 8 * 1024 * 1024

BF16 = jnp.bfloat16
F32 = jnp.float32


def _cparams(*sem):
    return pltpu.CompilerParams(dimension_semantics=sem, vmem_limit_bytes=VMEM_LIMIT_BYTES)


def _rms_rows(x, g, eps):
    ms = jnp.mean(x * x, axis=-1, keepdims=True)
    return x * lax.rsqrt(ms + eps) * g


def _inproj_kernel(x_ref, g_ref, w_ref, ws_ref, o_ref, os_ref, h_scr):
    @pl.when(pl.program_id(1) == 0)
    def _():
        h_scr[...] = _rms_rows(x_ref[...], g_ref[...], RMS_EPS).astype(BF16)
        os_ref[...] = jnp.dot(h_scr[...], ws_ref[...], preferred_element_type=F32)

    o_ref[...] = jnp.dot(h_scr[...], w_ref[...], preferred_element_type=F32)


def _inproj(xt, g, wq, n, tm=1024, tn=512):
    t, d = xt.shape
    ns = V7X_LANES
    return pl.pallas_call(
        _inproj_kernel,
        out_shape=(jax.ShapeDtypeStruct((t, n), F32), jax.ShapeDtypeStruct((t, ns), F32)),
        grid=(t // tm, n // tn),
        in_specs=[pl.BlockSpec((tm, d), lambda i, j: (i, 0)),
                  pl.BlockSpec((1, d), lambda i, j: (0, 0)),
                  pl.BlockSpec((d, tn), lambda i, j: (0, j)),
                  pl.BlockSpec((d, ns), lambda i, j: (0, n // ns))],
        out_specs=(pl.BlockSpec((tm, tn), lambda i, j: (i, j)), pl.BlockSpec((tm, ns), lambda i, j: (i, 0))),
        scratch_shapes=[pltpu.VMEM((tm, d), BF16)],
        compiler_params=_cparams("parallel", "arbitrary"),
        name="inproj",
    )(xt, g, wq, wq)


def _diffattn_kernel(relb_ref, q_ref, k_ref, v_ref, bucket_ref, lamqk_ref, g_ref, o_ref, *,
                     tq, lam_init, head_dim):
    h = pl.program_id(1)
    seq = q_ref.shape[0]
    margin = REL_MAX_DIST
    bucket = bucket_ref[...]
    band = jnp.zeros(bucket.shape, F32)
    for b in range(REL_BUCKETS):
        band = jnp.where(bucket == b, relb_ref[h, b], band)
    c_neg = band[0:1, 0:1]
    c_pos = band[tq - 1:tq, tq + 2 * margin - 1:tq + 2 * margin]

    lq = lamqk_ref[...]
    lam = (jnp.exp(jnp.sum(lq[0:1] * lq[1:2], axis=-1, keepdims=True))
           - jnp.exp(jnp.sum(lq[2:3] * lq[3:4], axis=-1, keepdims=True)) + lam_init)

    lane = lax.broadcasted_iota(jnp.int32, (1, 2 * head_dim), 1)
    lo_mask = (lane < head_dim).astype(F32)
    hi_mask = 1.0 - lo_mask
    k = k_ref[...].astype(BF16)
    v_aug = jnp.concatenate([v_ref[...].astype(BF16), jnp.ones((seq, 2 * head_dim), BF16)], axis=1)
    hw = 2 * head_dim
    scale = head_dim ** -0.5
    dn = (((1,), (1,)), ((), ()))
    for i in range(seq // tq):
        q0 = i * tq
        q = q_ref[q0:q0 + tq, :] * scale
        q1 = (q * lo_mask).astype(BF16)
        q2 = (q * hi_mask).astype(BF16)
        lo = max(q0 - margin, 0)
        hi = min(q0 + tq + margin, seq)
        pieces = []
        if lo > 0:
            pieces.append(jnp.broadcast_to(c_neg, (tq, lo)))
        pieces.append(band[:, lo - (q0 - margin):hi - (q0 - margin)])
        if hi < seq:
            pieces.append(jnp.broadcast_to(c_pos, (tq, seq - hi)))
        bias = jnp.concatenate(pieces, axis=1) if len(pieces) > 1 else pieces[0]
        s1 = lax.dot_general(q1, k, dn, preferred_element_type=F32) + bias
        s2 = lax.dot_general(q2, k, dn, preferred_element_type=F32) + bias
        p1 = jnp.exp(s1 - jnp.max(s1, axis=-1, keepdims=True)).astype(BF16)
        p2 = jnp.exp(s2 - jnp.max(s2, axis=-1, keepdims=True)).astype(BF16)
        pv1 = jnp.dot(p1, v_aug, preferred_element_type=F32)
        pv2 = jnp.dot(p2, v_aug, preferred_element_type=F32)
        o = pv1[:, :hw] / pv1[:, hw:hw + 1] - pv2[:, :hw] * (lam / pv2[:, hw:hw + 1])
        o = _rms_rows(o, g_ref[...], 1e-5) * (1.0 - lam_init)
        o_ref[q0:q0 + tq, :] = o


def _diff_attention(proj, col0, bsz, seq, lam_qk, subln_g, rel_bias, lam_init, tq=256):
    hw = 2 * DA_HD
    cb = col0 // hw
    margin = REL_MAX_DIST
    r = jnp.arange(tq)[:, None]
    c = jnp.arange(tq + 2 * margin)[None, :]
    bucket = _t5_bucket(c - margin - r)
    kern = partial(_diffattn_kernel, tq=tq, lam_init=lam_init, head_dim=DA_HD)
    return pl.pallas_call(
        kern,
        out_shape=jax.ShapeDtypeStruct((bsz * seq, DA_HEADS * hw), F32),
        grid=(bsz, DA_HEADS),
        in_specs=[pl.BlockSpec(memory_space=pltpu.SMEM),
                  pl.BlockSpec((seq, hw), lambda b, h: (b, cb + h)),
                  pl.BlockSpec((seq, hw), lambda b, h: (b, cb + DA_HEADS + h)),
                  pl.BlockSpec((seq, hw), lambda b, h: (b, cb + 2 * DA_HEADS + h)),
                  pl.BlockSpec(bucket.shape, lambda b, h: (0, 0)),
                  pl.BlockSpec(lam_qk.shape, lambda b, h: (0, 0)),
                  pl.BlockSpec((1, hw), lambda b, h: (0, 0))],
        out_specs=pl.BlockSpec((seq, hw), lambda b, h: (b, h)),
        compiler_params=_cparams("parallel", "parallel"),
        name="diffattn",
    )(rel_bias.T, proj, proj, proj, bucket, lam_qk, subln_g.reshape(1, hw))


def _merge_kernel(x_ref, g_ref, wg0_ref, wg1_ref, wg2_ref, wg3_ref, bg_ref, b0_ref, b1_ref, b2_ref, b3_ref,
                  wb_ref, wo_ref, o_ref, h_scr, br_scr, acc_scr, *, shift):
    j = pl.program_id(1)
    wg_ref = (wg0_ref, wg1_ref, wg2_ref, wg3_ref)

    @pl.when(j == 0)
    def _():
        h_scr[...] = _rms_rows(x_ref[...], g_ref[...], RMS_EPS).astype(BF16)
        for kk, b_ref in enumerate((b0_ref, b1_ref, b2_ref, b3_ref)):
            br_scr[kk] = b_ref[...].astype(BF16)
        acc_scr[...] = jnp.zeros_like(acc_scr)

    h = h_scr[...]
    tc = wo_ref.shape[0]
    c_pos = j * tc + lax.broadcasted_iota(jnp.int32, (1, tc), 1)
    valid = (c_pos >= shift) & (c_pos < shift + x_ref.shape[1])
    m = None
    for kk in range(N_BRANCH):
        gate = jnp.dot(h, wg_ref[kk][...], preferred_element_type=F32) + bg_ref[kk:kk + 1, :]
        bp = jnp.dot(br_scr[kk], wb_ref[kk], preferred_element_type=F32)
        term = jax.nn.sigmoid(gate) * bp
        m = term if m is None else m + term
    m = jnp.where(valid, m, 0.0)
    acc_scr[...] += jnp.dot(m.astype(BF16), wo_ref[...], preferred_element_type=F32)

    @pl.when(j == pl.num_programs(1) - 1)
    def _():
        o_ref[...] = x_ref[...] + acc_scr[...]


def _gated_merge(xt, g, wq, gate_col0, bg, branches, wb, wo, tm=512, tc=256):
    t, d = xt.shape
    bw = branches[0].shape[1]
    shift = gate_col0 % tc
    blk0 = gate_col0 // tc
    ncb = d // tc + 1
    dp = ncb * tc
    bg_s = jnp.pad(bg, ((0, 0), (shift, dp - d - shift)))
    wb_s = jnp.pad(wb, ((0, 0), (0, 0), (shift, dp - d - shift)))
    wo_s = jnp.pad(wo, ((shift, dp - d - shift), (0, 0)))
    return pl.pallas_call(
        partial(_merge_kernel, shift=shift),
        out_shape=jax.ShapeDtypeStruct((t, d), F32),
        grid=(t // tm, ncb),
        in_specs=[pl.BlockSpec((tm, d), lambda i, j: (i, 0)),
                  pl.BlockSpec((1, d), lambda i, j: (0, 0))]
                 + [pl.BlockSpec((d, tc), lambda i, j, kk=kk: (0, blk0 + kk * (d // tc) + j))
                    for kk in range(N_BRANCH)]
                 + [pl.BlockSpec((N_BRANCH, tc), lambda i, j: (0, j))]
                 + [pl.BlockSpec((tm, bw), lambda i, j: (i, 0))] * N_BRANCH
                 + [pl.BlockSpec((N_BRANCH, bw, tc), lambda i, j: (0, 0, j)),
                    pl.BlockSpec((tc, d), lambda i, j: (j, 0))],
        out_specs=pl.BlockSpec((tm, d), lambda i, j: (i, 0)),
        scratch_shapes=[pltpu.VMEM((tm, d), BF16), pltpu.VMEM((N_BRANCH, tm, bw), BF16),
                        pltpu.VMEM((tm, d), F32)],
        compiler_params=_cparams("parallel", "arbitrary"),
        name="gated_merge",
    )(xt, g, wq, wq, wq, wq, bg_s, *branches, wb_s, wo_s)


def _pack_bf16_halves(h):
    half = h.shape[1] // 2
    bits = lax.bitcast_convert_type(h.astype(F32), jnp.uint32)
    return (bits[:, :half] >> 16) | (bits[:, half:] & jnp.uint32(0xFFFF0000))


def _unpack_bf16_halves(p):
    lo = lax.bitcast_convert_type(p << 16, F32).astype(BF16)
    hi = lax.bitcast_convert_type(p & jnp.uint32(0xFFFF0000), F32).astype(BF16)
    return lo, hi


def _router_kernel(x_ref, g_ref, wr_ref, h_ref, aff_ref, *, n_experts):
    h = _rms_rows(x_ref[...], g_ref[...], RMS_EPS).astype(BF16)
    h_ref[...] = _pack_bf16_halves(h)
    logits = jnp.dot(h, wr_ref[...], preferred_element_type=F32)
    lane = lax.broadcasted_iota(jnp.int32, logits.shape, 1)
    logits = jnp.where(lane < n_experts, logits, -jnp.inf)
    p = jnp.exp(logits - jnp.max(logits, axis=-1, keepdims=True))
    aff_ref[...] = p / jnp.sum(p, axis=-1, keepdims=True)


def _norm_router(xt, g, w_router, tm=512):
    t, d = xt.shape
    e = w_router.shape[1]
    wr = jnp.zeros((d, V7X_LANES), BF16).at[:, :e].set(w_router.astype(BF16))
    return pl.pallas_call(
        partial(_router_kernel, n_experts=e),
        out_shape=(jax.ShapeDtypeStruct((t, d // 2), jnp.uint32), jax.ShapeDtypeStruct((t, V7X_LANES), F32)),
        grid=(t // tm,),
        in_specs=[pl.BlockSpec((tm, d), lambda i: (i, 0)),
                  pl.BlockSpec((1, d), lambda i: (0, 0)),
                  pl.BlockSpec((d, V7X_LANES), lambda i: (0, 0))],
        out_specs=(pl.BlockSpec((tm, d // 2), lambda i: (i, 0)),
                   pl.BlockSpec((tm, V7X_LANES), lambda i: (i, 0))),
        compiler_params=_cparams("parallel"),
        name="norm_router",
    )(xt, g, wr)


def _expert_kernel(xg_ref, w1_ref, w3_ref, w2_ref, gate_ref, o_ref, x_scr, hid_scr, *, nf):
    j = pl.program_id(1)
    half = x_scr.shape[1] // 2
    tf = w1_ref.shape[3]

    @pl.when(j == 0)
    def _():
        c = x_scr.shape[0]
        lanes = xg_ref.shape[2]
        sub = xg_ref.shape[1] // c
        for s in range(sub):
            lo, hi = _unpack_bf16_halves(xg_ref[0, pl.ds(s, c, stride=sub), :])
            x_scr[:, s * lanes:(s + 1) * lanes] = lo
            x_scr[:, half + s * lanes:half + (s + 1) * lanes] = hi

    @pl.when(j < nf)
    def _():
        xg = x_scr[...]
        a = jnp.dot(xg, w1_ref[0, 0].astype(BF16), preferred_element_type=F32)
        b = jnp.dot(xg, w3_ref[0, 0].astype(BF16), preferred_element_type=F32)
        hid_scr[:, pl.ds(pl.multiple_of(j * tf, tf), tf)] = (a * jax.nn.sigmoid(a) * b).astype(BF16)

    @pl.when(j >= nf)
    def _():
        y = jnp.dot(hid_scr[...], w2_ref[0, 0].astype(BF16), preferred_element_type=F32)
        o_ref[0] = (y * gate_ref[0]).astype(BF16)


def _experts(xg, w1, w3, w2, layer, gate, tf=256):
    e, rows, lanes = xg.shape
    c = gate.shape[1]
    d = w1.shape[2]
    f = w1.shape[3]
    nf = f // tf
    nd = d // tf
    up = lambda i, j: (layer, i, 0, jnp.minimum(j, nf - 1))
    down = lambda i, j: (layer, i, 0, jnp.maximum(j - nf, 0))
    return pl.pallas_call(
        partial(_expert_kernel, nf=nf),
        out_shape=jax.ShapeDtypeStruct((e, c, d), BF16),
        grid=(e, nf + nd),
        in_specs=[pl.BlockSpec((1, rows, lanes), lambda i, j: (i, 0, 0)),
                  pl.BlockSpec((1, 1, d, tf), up),
                  pl.BlockSpec((1, 1, d, tf), up),
                  pl.BlockSpec((1, 1, f, tf), down),
                  pl.BlockSpec((1, c, 1), lambda i, j: (i, 0, 0))],
        out_specs=pl.BlockSpec((1, c, tf), lambda i, j: (i, 0, jnp.maximum(j - nf, 0))),
        scratch_shapes=[pltpu.VMEM((c, d), BF16), pltpu.VMEM((c, f), BF16)],
        compiler_params=_cparams("parallel", "arbitrary"),
        name="experts",
    )(xg, w1, w3, w2, gate)


def _prefix_count(mask_f, tri):
    rows, n = mask_f.shape
    w = tri.shape[0]
    run = jnp.zeros((rows, 1), F32)
    outs = []
    for c in range(n // w):
        blk = mask_f[:, c * w:(c + 1) * w]
        outs.append(jnp.dot(blk.astype(BF16), tri, preferred_element_type=F32) + run)
        run = run + jnp.sum(blk, axis=-1, keepdims=True)
    return jnp.concatenate(outs, axis=1), run


def _topk_kernel(aff_ref, idx_ref, gate_ref, *, cap):
    aff = aff_ref[0]
    n_exp, n_tok = aff.shape
    keys = lax.bitcast_convert_type(aff, jnp.int32)
    thr = jnp.zeros((n_exp, 1), jnp.int32)
    for bit in range(30, -1, -1):
        cand = thr | (1 << bit)
        cnt = jnp.sum((keys >= cand).astype(F32), axis=-1, keepdims=True)
        thr = jnp.where(cnt >= cap, cand, thr)
    gt = (keys > thr).astype(F32)
    eq = (keys == thr).astype(F32)
    w = V7X_LANES
    r_i = lax.broadcasted_iota(jnp.int32, (w, w), 0)
    c_i = lax.broadcasted_iota(jnp.int32, (w, w), 1)
    tri = (r_i < c_i).astype(BF16)
    need = cap - jnp.sum(gt, axis=-1, keepdims=True)
    eq_rank, _ = _prefix_count(eq, tri)
    sel = gt + eq * (eq_rank < need).astype(F32)
    pos, _ = _prefix_count(sel, tri)
    tok = lax.broadcasted_iota(jnp.int32, (1, n_tok), 1)
    a_h = aff.astype(BF16)
    rem = aff - a_h.astype(F32)
    a_m = rem.astype(BF16)
    a_l = (rem - a_m.astype(F32)).astype(BF16)
    slot = lax.broadcasted_iota(jnp.int32, (cap, n_tok), 0).astype(F32)
    pos = jnp.where(sel > 0.5, pos, -1.0)
    dn = (((1,), (1,)), ((), ()))
    for e in range(n_exp):
        hit = jnp.where(pos[e:e + 1, :] == slot, 1.0, 0.0).astype(BF16)
        src = jnp.concatenate([(tok >> 6).astype(F32), (tok & 63).astype(F32),
                               a_h[e:e + 1, :].astype(F32), a_m[e:e + 1, :].astype(F32),
                               a_l[e:e + 1, :].astype(F32), jnp.zeros((3, n_tok), F32)], axis=0).astype(BF16)
        res = lax.dot_general(src, hit, dn, preferred_element_type=F32)
        idx_ref[0, e:e + 1, :] = (res[0:1] * 64.0 + res[1:2]).astype(jnp.int32)
        gate_ref[0, e:e + 1, :] = res[2:3] + res[3:4] + res[4:5]


def _topk_select(aff_t, cap):
    bsz, n_exp, n_tok = aff_t.shape
    blk = lambda n: pl.BlockSpec((1, n_exp, n), lambda b: (b, 0, 0))
    return pl.pallas_call(
        partial(_topk_kernel, cap=cap),
        out_shape=(jax.ShapeDtypeStruct((bsz, n_exp, cap), jnp.int32),
                   jax.ShapeDtypeStruct((bsz, n_exp, cap), F32)),
        grid=(bsz,),
        in_specs=[blk(n_tok)],
        out_specs=(blk(cap), blk(cap)),
        compiler_params=_cparams("parallel"),
        name="topk_select",
    )(aff_t)


def _gather_kernel(rows_ref, h_hbm, o_ref, sem):
    e = pl.program_id(0)
    sub = h_hbm.shape[1]
    n = o_ref.shape[1] // sub

    def start(j, carry):
        dst = o_ref.at[0, pl.ds(pl.multiple_of(j * sub, sub), sub)]
        pltpu.make_async_copy(h_hbm.at[rows_ref[e, j]], dst, sem).start()
        return carry

    lax.fori_loop(0, n, start, 0, unroll=8)
    pltpu.make_async_copy(o_ref.at[0], o_ref.at[0], sem).wait()


def _moe_gather(h, rows):
    n_exp, c = rows.shape
    _, sub, lanes = h.shape
    return pl.pallas_call(
        _gather_kernel,
        out_shape=jax.ShapeDtypeStruct((n_exp, c * sub, lanes), h.dtype),
        grid_spec=pltpu.PrefetchScalarGridSpec(
            num_scalar_prefetch=1,
            grid=(n_exp,),
            in_specs=[pl.BlockSpec(memory_space=pl.ANY)],
            out_specs=pl.BlockSpec((1, c * sub, lanes), lambda e, rows: (e, 0, 0)),
            scratch_shapes=[pltpu.SemaphoreType.DMA(())]),
        compiler_params=_cparams("arbitrary"),
        name="moe_gather",
    )(rows, h)


def _combine_kernel(idx_ref, x_ref, y_ref, o_ref):
    tq = x_ref.shape[0]
    q0 = pl.program_id(2) * tq
    n_exp, cap, td = y_ref.shape
    tokens = q0 + lax.broadcasted_iota(jnp.int32, (tq, 1), 0)
    hit = jnp.where(idx_ref[0] == tokens, 1.0, 0.0).astype(BF16)
    y = y_ref[...].reshape(n_exp * cap, td)
    o_ref[...] = x_ref[...] + jnp.dot(hit, y, preferred_element_type=F32)


def _moe_combine(xt, y, idx, bsz, tq=512, td=512):
    t, d = xt.shape
    n_tok = t // bsz
    n_exp, _, cap = idx.shape[1], None, idx.shape[2]
    idx_flat = idx.reshape(bsz, 1, n_exp * cap)
    tq = min(tq, n_tok)
    td = min(td, d)
    nq = n_tok // tq
    return pl.pallas_call(
        _combine_kernel,
        out_shape=jax.ShapeDtypeStruct((t, d), F32),
        grid=(bsz, d // td, nq),
        in_specs=[pl.BlockSpec((1, 1, n_exp * cap), lambda b, j, q: (b, 0, 0)),
                  pl.BlockSpec((tq, td), lambda b, j, q: (b * nq + q, j)),
                  pl.BlockSpec((n_exp, cap, td), lambda b, j, q: (0, b, j))],
        out_specs=pl.BlockSpec((tq, td), lambda b, j, q: (b * nq + q, j)),
        compiler_params=_cparams("parallel", "parallel", "arbitrary"),
        name="moe_combine",
    )(idx_flat, xt, y)


def _expert_choice_ffn(xt, bsz, n_tok, norm_g, w_router, w1, w3, w2, layer):
    d = xt.shape[1]
    n_exp = w_router.shape[1]
    cap = EC_CAPACITY * n_tok // n_exp
    h2, aff = _norm_router(xt, norm_g.reshape(1, d), w_router)
    aff_t = jnp.swapaxes(aff[:, :n_exp].reshape(bsz, n_tok, n_exp), 1, 2)
    idx, gate = _topk_select(aff_t, cap)
    rows = idx + (jnp.arange(bsz, dtype=jnp.int32) * n_tok)[:, None, None]
    rows = jnp.swapaxes(rows, 0, 1).reshape(n_exp, bsz * cap)
    gate = jnp.swapaxes(gate, 0, 1).reshape(n_exp, bsz * cap, 1)
    xg = _moe_gather(h2.reshape(h2.shape[0], -1, V7X_LANES), rows)
    y = _experts(xg, w1, w3, w2, layer, gate)
    return _moe_combine(xt, y, idx, bsz)


def _final_norm_kernel(x_ref, g_ref, o_ref):
    o_ref[...] = _rms_rows(x_ref[...], g_ref[...], RMS_EPS)


def _final_norm(xt, g, tm=512):
    t, d = xt.shape
    return pl.pallas_call(
        _final_norm_kernel,
        out_shape=jax.ShapeDtypeStruct(xt.shape, xt.dtype),
        grid=(t // tm,),
        in_specs=[pl.BlockSpec((tm, d), lambda i: (i, 0)), pl.BlockSpec((1, d), lambda i: (0, 0))],
        out_specs=pl.BlockSpec((tm, d), lambda i: (i, 0)),
        compiler_params=_cparams("parallel"),
        name="final_norm",
    )(xt, g.reshape(1, d))


def _shift_rows(x, offset):
    n = x.shape[0]
    if offset == 0:
        return x
    row = lax.broadcasted_iota(jnp.int32, x.shape, 0)
    rolled = pltpu.roll(x, (-offset) % n, axis=0)
    valid = (row + offset >= 0) & (row + offset < n)
    return jnp.where(valid, rolled, 0.0)


def _dwconv_rows(x, w_ref, b_ref):
    taps = w_ref.shape[0]
    y = b_ref[...] + jnp.zeros_like(x)
    for kk in range(taps):
        y = y + w_ref[kk:kk + 1, :] * _shift_rows(x, kk - taps // 2)
    return y


def _gelu_tanh(x):
    return 0.5 * x * (1.0 + jnp.tanh(math.sqrt(2.0 / math.pi) * (x + 0.044715 * (x * x * x))))


def _rglru_kernel(x_ref, gate_ref, cw_ref, cb_ref, wa_ref, ba_ref, wx_ref, bx_ref, sp_ref, o_ref,
                  af_scr, bf_scr, ab_scr, bb_scr):
    nb, seq, _ = x_ref.shape
    row = lax.broadcasted_iota(jnp.int32, (seq, x_ref.shape[2]), 0)
    for b in range(nb):
        xc = _dwconv_rows(x_ref[b], cw_ref, cb_ref)
        xcb = xc.astype(BF16)
        for dr, (a_scr, b_scr) in enumerate(((af_scr, bf_scr), (ab_scr, bb_scr))):
            r = jax.nn.sigmoid(jnp.dot(xcb, wa_ref[dr, 0], preferred_element_type=F32) + ba_ref[dr:dr + 1, :])
            i = jax.nn.sigmoid(jnp.dot(xcb, wx_ref[dr, 0], preferred_element_type=F32) + bx_ref[dr:dr + 1, :])
            log_a = -LRU_C * r * sp_ref[dr:dr + 1, :]
            mult = jnp.sqrt(1.0 - jnp.exp(2.0 * log_a))
            mult = jnp.where(row == (seq - 1 if dr else 0), 1.0, mult)
            a_scr[b] = jnp.exp(log_a)
            b_scr[b] = mult * i * xc

    def step(tt, carry):
        new = []
        tb = seq - 1 - tt
        for b in range(nb):
            hf, hb = carry[b]
            hf = af_scr[b, pl.ds(tt, 1), :] * hf + bf_scr[b, pl.ds(tt, 1), :]
            hb = ab_scr[b, pl.ds(tb, 1), :] * hb + bb_scr[b, pl.ds(tb, 1), :]
            bf_scr[b, pl.ds(tt, 1), :] = hf
            bb_scr[b, pl.ds(tb, 1), :] = hb
            new.append((hf, hb))
        return tuple(new)

    zero = jnp.zeros((1, x_ref.shape[2]), F32)
    lax.fori_loop(0, seq, step, tuple((zero, zero) for _ in range(nb)), unroll=8)
    for b in range(nb):
        o_ref[b] = (bf_scr[b] + bb_scr[b]) * _gelu_tanh(gate_ref[b])


def _block_diag_tiles(w, tile):
    nd, nh, hd, _ = w.shape
    per = tile // hd
    w = w.reshape(nd, nh // per, per, hd, hd)
    eye = jnp.eye(per, dtype=w.dtype)
    bd = jnp.einsum('dgpij,pq->dgpiqj', w, eye).reshape(nd, nh // per, tile, tile)
    return bd.astype(BF16)


def _rglru_mixer(proj3, col_x, col_gate, conv_w, conv_b, w_a, b_a, w_x, b_x, lam):
    bsz, seq, _ = proj3.shape
    tc = V7X_LANES
    nct = LRU_W // tc
    sp = jax.nn.softplus(-lam.astype(F32))
    wa = _block_diag_tiles(w_a, tc)
    wx = _block_diag_tiles(w_x, tc)
    vec = lambda i: (0, i)
    return pl.pallas_call(
        _rglru_kernel,
        out_shape=jax.ShapeDtypeStruct((bsz, seq, LRU_W), F32),
        grid=(nct,),
        in_specs=[pl.BlockSpec((bsz, seq, tc), lambda i: (0, 0, col_x // tc + i)),
                  pl.BlockSpec((bsz, seq, tc), lambda i: (0, 0, col_gate // tc + i)),
                  pl.BlockSpec((LRU_CONV, tc), vec),
                  pl.BlockSpec((1, tc), vec),
                  pl.BlockSpec((2, 1, tc, tc), lambda i: (0, i, 0, 0)),
                  pl.BlockSpec((2, tc), vec),
                  pl.BlockSpec((2, 1, tc, tc), lambda i: (0, i, 0, 0)),
                  pl.BlockSpec((2, tc), vec),
                  pl.BlockSpec((2, tc), vec)],
        out_specs=pl.BlockSpec((bsz, seq, tc), lambda i: (0, 0, i)),
        scratch_shapes=[pltpu.VMEM((bsz, seq, tc), F32)] * 4,
        compiler_params=_cparams("parallel"),
        name="rglru",
    )(proj3, proj3, conv_w, conv_b.reshape(1, -1), wa, b_a, wx, b_x, sp)


def _dft_tables(seq):
    n = 3 * seq // 2
    kf = np.arange(n // 2, dtype=np.int64)[:, None]
    s = np.arange(seq, dtype=np.int64)[None, :]
    ang = (np.pi / n) * (((2 * kf + 1) * s) % (2 * n)).astype(np.float64)
    fc = np.cos(ang)
    fs = -np.sin(ang)
    shift = (np.pi / n) * (((2 * kf + 1) * (seq // 2)) % (2 * n)).astype(np.float64)
    to_bf16 = lambda a: jnp.asarray(a.astype(np.float32)).astype(BF16)
    return dict(n=n, fc=to_bf16(fc), fs=to_bf16(fs), fct=to_bf16(fc.T), fst=to_bf16(fs.T),
                pc=jnp.asarray(np.cos(shift).astype(np.float32)), ps=jnp.asarray(np.sin(shift).astype(np.float32)))


def _hyena_pre_kernel(x0_ref, x1_ref, v_ref, w0_ref, w1_ref, w2_ref, b0_ref, b1_ref, b2_ref,
                      vp_ref, vpb_ref, x0c_ref):
    x0c_ref[0] = _dwconv_rows(x0_ref[0], w0_ref, b0_ref)
    vp = _dwconv_rows(v_ref[0], w2_ref, b2_ref) * _dwconv_rows(x1_ref[0], w1_ref, b1_ref)
    vp_ref[0] = vp
    vpb_ref[0] = vp.astype(BF16)


def _hyena_filter_kernel(z_ref, t_ref, dl_ref, w1_ref, b1_ref, w2_ref, b2_ref, w3_ref, b3_ref, w4_ref, fr_ref,
                         fc_ref, fs_ref, pc_ref, ps_ref, hr_ref, hi_ref, filt_scr):
    @pl.when(pl.program_id(0) == 0)
    def _():
        fr = fr_ref[...]
        hid = jnp.sin(fr * (jnp.dot(z_ref[...].astype(BF16), w1_ref[...].astype(BF16),
                                    preferred_element_type=F32) + b1_ref[...]))
        hid = jnp.sin(fr * (jnp.dot(hid.astype(BF16), w2_ref[...].astype(BF16),
                                    preferred_element_type=F32) + b2_ref[...]))
        hid = jnp.sin(fr * (jnp.dot(hid.astype(BF16), w3_ref[...].astype(BF16),
                                    preferred_element_type=F32) + b3_ref[...]))
        filt = jnp.dot(hid.astype(BF16), w4_ref[...].astype(BF16), preferred_element_type=F32)
        filt = filt * jnp.exp(-2.0 * jnp.abs(t_ref[...] - 0.5) * dl_ref[...])
        filt = filt / jnp.sum(jnp.abs(filt), axis=0, keepdims=True)
        filt_scr[...] = filt.astype(BF16)

    f = filt_scr[...]
    hr0 = jnp.dot(fc_ref[...], f, preferred_element_type=F32)
    hi0 = jnp.dot(fs_ref[...], f, preferred_element_type=F32)
    pc = pc_ref[...]
    ps = ps_ref[...]
    hr_ref[...] = hr0 * pc - hi0 * ps
    hi_ref[...] = hr0 * ps + hi0 * pc


def _hyena_fwd_kernel(fc_ref, fs_ref, vp_ref, hr_ref, hi_ref, yr_ref, yi_ref):
    v = vp_ref[0]
    vr = jnp.dot(fc_ref[...], v, preferred_element_type=F32)
    vi = jnp.dot(fs_ref[...], v, preferred_element_type=F32)
    hr = hr_ref[...]
    hi = hi_ref[...]
    yr_ref[0] = (vr * hr - vi * hi).astype(BF16)
    yi_ref[0] = (vr * hi + vi * hr).astype(BF16)


def _hyena_inv_kernel(fct_ref, fst_ref, yr_ref, yi_ref, vp_ref, x0c_ref, bias_ref, o_ref, *, inv_scale):
    y = (jnp.dot(fct_ref[...], yr_ref[0], preferred_element_type=F32)
         + jnp.dot(fst_ref[...], yi_ref[0], preferred_element_type=F32)) * inv_scale
    o_ref[0] = (y + vp_ref[0] * bias_ref[...]) * x0c_ref[0]


def _hyena_mixer_pallas(proj3, col0, conv_w, conv_b, fw1, fb1, fw2, fb2, fw3, fb3, fw4, freq, fft_bias,
                        tk=512, tt=512):
    bsz, seq, _ = proj3.shape
    w = fw4.shape[1]
    tc = V7X_LANES
    nct = w // tc
    cb0 = col0 // tc
    tab = _dft_tables(seq)
    nfreq = tab['n'] // 2
    cwb = lambda off: pl.BlockSpec((HY_SHORT, tc), lambda b, c: (0, off + c))
    cbb = lambda off: pl.BlockSpec((1, tc), lambda b, c: (0, off + c))
    xb = lambda off: pl.BlockSpec((1, seq, tc), lambda b, c: (b, 0, cb0 + off + c))
    ob = pl.BlockSpec((1, seq, tc), lambda b, c: (b, 0, c))
    cb2 = conv_b.reshape(1, -1)
    vp, vpb, x0c = pl.pallas_call(
        _hyena_pre_kernel,
        out_shape=(jax.ShapeDtypeStruct((bsz, seq, w), F32), jax.ShapeDtypeStruct((bsz, seq, w), BF16),
                   jax.ShapeDtypeStruct((bsz, seq, w), F32)),
        grid=(bsz, nct),
        in_specs=[xb(0), xb(nct), xb(2 * nct), cwb(0), cwb(nct), cwb(2 * nct), cbb(0), cbb(nct), cbb(2 * nct)],
        out_specs=(ob, ob, ob),
        compiler_params=_cparams("parallel", "parallel"),
        name="hyena_pre",
    )(proj3, proj3, proj3, conv_w, conv_w, conv_w, cb2, cb2, cb2)

    tcol = np.linspace(0.0, 1.0, seq, dtype=np.float32)[:, None]
    wcol = ((2.0 * math.pi / seq) * np.arange(seq, dtype=np.float32))[:, None]
    bands = np.linspace(1e-4, HY_BANDS - 1, HY_BANDS, dtype=np.float32)[None, :]
    z = np.concatenate([tcol, np.cos(bands * wcol), -np.sin(bands * wcol)], axis=-1).astype(np.float32)
    zp = np.zeros((seq, tc), np.float32)
    zp[:, :HY_EMB] = z
    w1p = jnp.zeros((tc, HY_ORDER), F32).at[:HY_EMB].set(fw1)
    deltas = np.abs(np.linspace(math.log(HY_TARGET) / HY_SLOW, math.log(HY_TARGET) / HY_FAST, w,
                                dtype=np.float32))[None, :]
    full = lambda a: pl.BlockSpec(a.shape, lambda j: (0,) * a.ndim)
    row = lambda a: a.reshape(1, -1)
    small = [jnp.asarray(zp), jnp.asarray(tcol), jnp.asarray(deltas), w1p, row(fb1), fw2, row(fb2), fw3, row(fb3),
             fw4, row(freq)]
    hr, hi = pl.pallas_call(
        _hyena_filter_kernel,
        out_shape=(jax.ShapeDtypeStruct((nfreq, w), F32), jax.ShapeDtypeStruct((nfreq, w), F32)),
        grid=(nfreq // tk,),
        in_specs=[full(a) for a in small]
                 + [pl.BlockSpec((tk, seq), lambda j: (j, 0)), pl.BlockSpec((tk, seq), lambda j: (j, 0)),
                    pl.BlockSpec((tk, 1), lambda j: (j, 0)), pl.BlockSpec((tk, 1), lambda j: (j, 0))],
        out_specs=(pl.BlockSpec((tk, w), lambda j: (j, 0)), pl.BlockSpec((tk, w), lambda j: (j, 0))),
        scratch_shapes=[pltpu.VMEM((seq, w), BF16)],
        compiler_params=_cparams("arbitrary"),
        name="hyena_filter",
    )(*small, tab['fc'], tab['fs'], tab['pc'], tab['ps'])

    yr, yi = pl.pallas_call(
        _hyena_fwd_kernel,
        out_shape=(jax.ShapeDtypeStruct((bsz, nfreq, w), BF16), jax.ShapeDtypeStruct((bsz, nfreq, w), BF16)),
        grid=(bsz, nfreq // tk),
        in_specs=[pl.BlockSpec((tk, seq), lambda b, j: (j, 0)), pl.BlockSpec((tk, seq), lambda b, j: (j, 0)),
                  pl.BlockSpec((1, seq, w), lambda b, j: (b, 0, 0)),
                  pl.BlockSpec((tk, w), lambda b, j: (j, 0)), pl.BlockSpec((tk, w), lambda b, j: (j, 0))],
        out_specs=(pl.BlockSpec((1, tk, w), lambda b, j: (b, j, 0)), pl.BlockSpec((1, tk, w), lambda b, j: (b, j, 0))),
        compiler_params=_cparams("parallel", "parallel"),
        name="hyena_fwd",
    )(tab['fc'], tab['fs'], vpb, hr, hi)

    return pl.pallas_call(
        partial(_hyena_inv_kernel, inv_scale=2.0 / tab['n']),
        out_shape=jax.ShapeDtypeStruct((bsz, seq, w), F32),
        grid=(bsz, seq // tt),
        in_specs=[pl.BlockSpec((tt, nfreq), lambda b, j: (j, 0)), pl.BlockSpec((tt, nfreq), lambda b, j: (j, 0)),
                  pl.BlockSpec((1, nfreq, w), lambda b, j: (b, 0, 0)), pl.BlockSpec((1, nfreq, w), lambda b, j: (b, 0, 0)),
                  pl.BlockSpec((1, tt, w), lambda b, j: (b, j, 0)), pl.BlockSpec((1, tt, w), lambda b, j: (b, j, 0)),
                  pl.BlockSpec((1, w), lambda b, j: (0, 0))],
        out_specs=pl.BlockSpec((1, tt, w), lambda b, j: (b, j, 0)),
        compiler_params=_cparams("parallel", "parallel"),
        name="hyena_inv",
    )(tab['fct'], tab['fst'], yr, yi, vp, x0c, fft_bias.reshape(1, w))


def _split3_bf16(x):
    hi = x.astype(BF16)
    r1 = x - hi.astype(F32)
    mid = r1.astype(BF16)
    lo = (r1 - mid.astype(F32)).astype(BF16)
    return hi, mid, lo


def _exact_tri_matmul(tri, x):
    hi, mid, lo = _split3_bf16(x)
    return (jnp.dot(tri, hi, preferred_element_type=F32) + jnp.dot(tri, mid, preferred_element_type=F32)
            + jnp.dot(tri, lo, preferred_element_type=F32))


def _silu(x):
    return x * jax.nn.sigmoid(x)


def _ssd_kernel(z_ref, xs_ref, bm_ref, cm_ref, dt_ref, cwx_ref, cbx_ref, cwb_ref, cbb_ref, cwc_ref, cbc_ref,
                dtb_ref, alog_ref, dsk_ref, ng_ref, o_ref, xs_scr, bm_scr, cm_scr, dt_scr, da_scr, yb_scr, st_scr,
                *, chunk, head_dim, heads):
    seq = xs_ref.shape[1]
    lanes = V7X_LANES
    n_chunks = seq // chunk
    pairs = heads * head_dim // lanes
    per = lanes // head_dim

    xs_scr[...] = _silu(_dwconv_rows(xs_ref[0], cwx_ref, cbx_ref))
    bm_scr[...] = _silu(_dwconv_rows(bm_ref[0], cwb_ref, cbb_ref))
    cm_scr[...] = _silu(_dwconv_rows(cm_ref[0], cwc_ref, cbc_ref))
    raw = dt_ref[0, 0] + dtb_ref[0]
    dt = jnp.maximum(raw, 0.0) + jnp.log(1.0 + jnp.exp(-jnp.abs(raw)))
    dt_scr[...] = dt
    da_scr[...] = dt * (-jnp.exp(alog_ref[0]))
    st_scr[...] = jnp.zeros_like(st_scr)

    r_i = lax.broadcasted_iota(jnp.int32, (chunk, chunk), 0)
    c_i = lax.broadcasted_iota(jnp.int32, (chunk, chunk), 1)
    tri_lo = (c_i <= r_i).astype(BF16)
    tri_up = (c_i >= r_i).astype(BF16)
    lane = lax.broadcasted_iota(jnp.int32, (1, lanes), 1)
    head_mask = [((lane >= hh * head_dim) & (lane < (hh + 1) * head_dim)).astype(F32) for hh in range(per)]

    def by_head(cols):
        out = cols[0] * head_mask[0]
        for hh in range(1, per):
            out = out + cols[hh] * head_mask[hh]
        return out

    def one_chunk(c, reverse):
        rows = pl.ds(pl.multiple_of(c * chunk, chunk), chunk)
        da = da_scr[rows, :]
        dtc = dt_scr[rows, :]
        cum = _exact_tri_matmul(tri_up if reverse else tri_lo, da)
        cum_t = cum.T
        edge = cum[0:1, :] if reverse else cum[chunk - 1:chunk, :]
        keep = (c_i >= r_i) if reverse else (c_i <= r_i)
        bmat = bm_scr[rows, :]
        cmat = cm_scr[rows, :].astype(BF16)
        cb = lax.dot_general(cmat, bmat.astype(BF16), (((1,), (1,)), ((), ())), preferred_element_type=F32)
        bmat_t = bmat.T.astype(BF16)
        off = heads if reverse else 0
        outs = []
        for p in range(pairs):
            xs = xs_scr[rows, p * lanes:(p + 1) * lanes]
            hs = [off + p * per + hh for hh in range(per)]
            y = None
            for hh, h in enumerate(hs):
                seg = cum[:, h:h + 1] - cum_t[h:h + 1, :]
                lmat = jnp.exp(jnp.where(keep, seg, -1e30))
                xd_h = (xs * dtc[:, h:h + 1] * head_mask[hh]).astype(BF16)
                term = jnp.dot((cb * lmat).astype(BF16), xd_h, preferred_element_type=F32)
                y = term if y is None else y + term
            sidx = (pairs if reverse else 0) + p
            state = st_scr[sidx]
            y = y + (jnp.dot(cmat, state.astype(BF16), preferred_element_type=F32)
                     * by_head([jnp.exp(cum[:, h:h + 1]) for h in hs]))
            xd = xs * by_head([dtc[:, h:h + 1] for h in hs])
            decay_s = by_head([jnp.exp(edge[:, h:h + 1] - cum[:, h:h + 1]) for h in hs])
            chunk_decay = by_head([jnp.exp(edge[:, h:h + 1]) for h in hs])
            st_scr[sidx] = chunk_decay * state + jnp.dot(bmat_t, (xd * decay_s).astype(BF16),
                                                         preferred_element_type=F32)
            outs.append(y)
        return rows, jnp.concatenate(outs, axis=1)

    def body(i, carry):
        rows_f, y_f = one_chunk(i, False)
        o_ref[0, rows_f, :] = y_f
        rows_b, y_b = one_chunk(n_chunks - 1 - i, True)
        yb_scr[rows_b, :] = y_b
        return carry

    lax.fori_loop(0, n_chunks, body, 0)

    y = (o_ref[0] + yb_scr[...] + dsk_ref[...] * xs_scr[...]) * _silu(z_ref[0])
    y = y * lax.rsqrt(jnp.mean(y * y, axis=-1, keepdims=True) + RMS_EPS)
    o_ref[0] = y * ng_ref[...]


def _ssd_mixer_pallas(proj3, col_z, dt_raw, conv_w, conv_b, dt_bias, a_log, d_skip, norm_g):
    bsz, seq, _ = proj3.shape
    lanes = V7X_LANES
    g = SSD_GROUPS
    hg = SSD_HEADS // g
    gw = SSD_W // g
    col_x = col_z + SSD_W
    col_b = col_x + SSD_W
    col_c = col_b + g * SSD_STATE

    def per_group(v):
        lead = v.shape[:-2]
        v = v.reshape(lead + (2, g, hg))
        v = jnp.moveaxis(v, -2, 0).reshape((g,) + lead + (2 * hg,))
        return jnp.pad(v, [(0, 0)] * (v.ndim - 1) + [(0, lanes - 2 * hg)])
    dtg = jnp.moveaxis(per_group(dt_raw[..., :2 * SSD_HEADS].reshape(bsz, seq, 2, SSD_HEADS)), 0, 1)
    dtb = per_group(dt_bias.astype(F32)).reshape(g, 1, lanes)
    alog = per_group(a_log.astype(F32)).reshape(g, 1, lanes)
    dsk = jnp.repeat(d_skip.astype(F32), SSD_HD).reshape(1, SSD_W)
    cb2 = conv_b.reshape(1, -1)
    nsb = SSD_W // SSD_STATE
    kern = partial(_ssd_kernel, chunk=SSD_CHUNK, head_dim=SSD_HD, heads=hg)
    return pl.pallas_call(
        kern,
        out_shape=jax.ShapeDtypeStruct((bsz, seq, SSD_W), F32),
        grid=(bsz, g),
        in_specs=[pl.BlockSpec((1, seq, gw), lambda b, gi: (b, 0, col_z // gw + gi)),
                  pl.BlockSpec((1, seq, gw), lambda b, gi: (b, 0, col_x // gw + gi)),
                  pl.BlockSpec((1, seq, SSD_STATE), lambda b, gi: (b, 0, col_b // SSD_STATE + gi)),
                  pl.BlockSpec((1, seq, SSD_STATE), lambda b, gi: (b, 0, col_c // SSD_STATE + gi)),
                  pl.BlockSpec((1, 1, seq, lanes), lambda b, gi: (b, gi, 0, 0)),
                  pl.BlockSpec((SSD_CONV, gw), lambda b, gi: (0, gi)),
                  pl.BlockSpec((1, gw), lambda b, gi: (0, gi)),
                  pl.BlockSpec((SSD_CONV, SSD_STATE), lambda b, gi: (0, nsb + gi)),
                  pl.BlockSpec((1, SSD_STATE), lambda b, gi: (0, nsb + gi)),
                  pl.BlockSpec((SSD_CONV, SSD_STATE), lambda b, gi: (0, nsb + g + gi)),
                  pl.BlockSpec((1, SSD_STATE), lambda b, gi: (0, nsb + g + gi)),
                  pl.BlockSpec((1, 1, lanes), lambda b, gi: (gi, 0, 0)),
                  pl.BlockSpec((1, 1, lanes), lambda b, gi: (gi, 0, 0)),
                  pl.BlockSpec((1, gw), lambda b, gi: (0, gi)),
                  pl.BlockSpec((1, gw), lambda b, gi: (0, gi))],
        out_specs=pl.BlockSpec((1, seq, gw), lambda b, gi: (b, 0, gi)),
        scratch_shapes=[pltpu.VMEM((seq, gw), F32), pltpu.VMEM((seq, SSD_STATE), F32),
                        pltpu.VMEM((seq, SSD_STATE), F32), pltpu.VMEM((seq, lanes), F32),
                        pltpu.VMEM((seq, lanes), F32), pltpu.VMEM((seq, gw), F32),
                        pltpu.VMEM((2 * gw // lanes, SSD_STATE, lanes), F32)],
        compiler_params=_cparams("parallel", "parallel"),
        name="ssd",
    )(proj3, proj3, proj3, proj3, dtg, conv_w, cb2, conv_w, cb2, conv_w, cb2, dtb, alog, dsk,
      norm_g.reshape(1, SSD_W))


def _t5_bucket(rel):
    nb = REL_BUCKETS // 2
    ret = (rel > 0).astype(jnp.int32) * nb
    n = jnp.abs(rel)
    max_exact = nb // 2
    large = max_exact + (jnp.log(jnp.maximum(n, 1).astype(jnp.float32) / max_exact)
                         / math.log(REL_MAX_DIST / max_exact) * (nb - max_exact)).astype(jnp.int32)
    large = jnp.minimum(large, nb - 1)
    return ret + jnp.where(n < max_exact, n, large)


MIX_COLS = HY_COLS + LRU_COLS + DA_COLS + SSD_W + SSD_XBC
DT_COLS = 2 * SSD_HEADS
MIX_PAD = MIX_COLS + V7X_LANES


def kernel(x, norm1_g, w_in, hy_conv_w, hy_conv_b, hy_fw1, hy_fb1, hy_fw2, hy_fb2, hy_fw3, hy_fb3,
           hy_fw4, hy_freq, hy_bias, lru_conv_w, lru_conv_b, lru_wa, lru_ba, lru_wx, lru_bx, lru_lam,
           da_lam, da_subln_g, ssd_conv_w, ssd_conv_b, ssd_dt_bias, ssd_a_log, ssd_d, ssd_norm_g,
           w_branch, b_gate, w_out, norm2_g, w_router, moe_w1, moe_w3, moe_w2, rel_bias, final_g):
    bsz, seq_len, d_model = x.shape
    t = bsz * seq_len
    xt = x.reshape(t, d_model)
    o1 = HY_COLS
    o2 = o1 + LRU_COLS
    o3 = o2 + DA_COLS
    o4 = o3 + SSD_W + SSD_XBC
    for l in range(DEPTH):
        g1 = norm1_g[l].reshape(1, d_model)
        wq = w_in[l].astype(BF16)
        proj, dt_raw = _inproj(xt, g1, wq, MIX_COLS)
        p3 = proj.reshape(bsz, seq_len, MIX_COLS)
        o_hy = _hyena_mixer_pallas(p3, 0, hy_conv_w[l], hy_conv_b[l], hy_fw1[l], hy_fb1[l], hy_fw2[l],
                                   hy_fb2[l], hy_fw3[l], hy_fb3[l], hy_fw4[l], hy_freq[l], hy_bias[l])
        o_lru = _rglru_mixer(p3, o1, o1 + LRU_W, lru_conv_w[l], lru_conv_b[l],
                             lru_wa[l], lru_ba[l], lru_wx[l], lru_bx[l], lru_lam[l])
        lam_init = 0.8 - 0.6 * math.exp(-0.3 * l)
        o_da = _diff_attention(proj, o2, bsz, seq_len, da_lam[l], da_subln_g[l], rel_bias, lam_init)
        o_ssd = _ssd_mixer_pallas(p3, o3, dt_raw.reshape(bsz, seq_len, V7X_LANES), ssd_conv_w[l], ssd_conv_b[l],
                                  ssd_dt_bias[l], ssd_a_log[l], ssd_d[l], ssd_norm_g[l])
        branches = [o_hy.reshape(t, BR_W), o_lru.reshape(t, BR_W), o_da, o_ssd.reshape(t, BR_W)]
        xt = _gated_merge(xt, g1, wq, MIX_COLS + DT_COLS, b_gate[l], branches, w_branch[l].astype(BF16),
                          w_out[l].astype(BF16))
        xt = _expert_choice_ffn(xt, bsz, seq_len, norm2_g[l], w_router[l], moe_w1, moe_w3, moe_w2, l)
    return _final_norm(xt, final_g).reshape(bsz, seq_len, d_model)
```
